```python
import math
import jax, jax.numpy as jnp
from jax import lax
import numpy as np

D_MODEL = 1024
BATCH = 8
SEQ = 8192
DEPTH = 2
DEC_BATCH = 16
DEC_SEQ = 16
PAST_LEN = 2048

CHUNK = 64
EPS = 1e-6
CONV_W = 4
A_WIDTH = D_MODEL // 4
B_WIDTH = D_MODEL // 2
C_WIDTH = D_MODEL - A_WIDTH - B_WIDTH
A_HEADS = 4
A_HEAD_DIM = A_WIDTH // A_HEADS
LRU_C = 8.0
B_HEAD_DIM = 64
B_HEADS = B_WIDTH // B_HEAD_DIM
B_GROUPS = 2
B_HPG = B_HEADS // B_GROUPS
B_STATE = 128
B_CONV_DIM = B_WIDTH + 2 * B_GROUPS * B_STATE
C_HEADS = 4
C_KDIM = C_WIDTH // C_HEADS
C_VDIM = C_WIDTH // C_HEADS
D_FF = ((-(-8 * D_MODEL // 3)) + 255) // 256 * 256
SPLIT_SIZES = (A_WIDTH, A_WIDTH, B_WIDTH, B_CONV_DIM, B_HEADS, C_WIDTH, C_WIDTH, C_WIDTH, C_WIDTH)
SPLIT_IDX = tuple(int(v) for v in np.cumsum(SPLIT_SIZES)[:-1])
IN_COLS = sum(SPLIT_SIZES)

kernel_name = "hymba_rglru_ssd_hgrn2_stream_step"


def rmsnorm(x, w):
    xf = x.astype(jnp.float32)
    y = xf * lax.rsqrt(jnp.mean(xf * xf, axis=-1, keepdims=True) + EPS)
    return (y * w.astype(jnp.float32)).astype(x.dtype)


def causal_conv(x, prev, w, b):
    t = x.shape[1]
    xp = jnp.concatenate([prev.astype(x.dtype), x], axis=1)
    out = b
    for k in range(CONV_W):
        out = out + xp[:, k:k + t] * w[k]
    return out, xp[:, t:]


def chunk_size(t):
    cs = min(CHUNK, t)
    assert t % cs == 0
    return cs


def to_chunks(a, cs):
    b, t = a.shape[:2]
    return a.reshape(b, t // cs, cs, *a.shape[2:]).swapaxes(0, 1)


def from_chunks(a):
    nc, b, cs = a.shape[:3]
    return a.swapaxes(0, 1).reshape(b, nc * cs, *a.shape[3:])


def causal_mask(l):
    return jnp.tril(jnp.ones((l, l), dtype=bool))


def rglru(x, h0, wa, ba, wx, bx, lam):
    f32 = jnp.float32
    b_, t = x.shape[:2]
    xh = x.reshape(b_, t, A_HEADS, A_HEAD_DIM)
    r = jax.nn.sigmoid((jnp.einsum("bthi,hij->bthj", xh, wa).reshape(b_, t, A_WIDTH) + ba).astype(f32))
    i = jax.nn.sigmoid((jnp.einsum("bthi,hij->bthj", xh, wx).reshape(b_, t, A_WIDTH) + bx).astype(f32))
    log_a = -LRU_C * r * jax.nn.softplus(-lam.astype(f32))
    a = jnp.exp(log_a)
    u = jnp.sqrt(-jnp.expm1(2.0 * log_a)) * (i * x.astype(f32))
    u = u.at[:, 0].add(a[:, 0] * h0.astype(f32))

    def combine(left, right):
        return (left[0] * right[0], right[0] * left[1] + right[1])

    _, h = lax.associative_scan(combine, (a, u), axis=1)
    return h, h[:, -1]


def ssd_chunk(s, inp):
    x, da, bm, cm = inp
    l = x.shape[1]
    acs = jnp.cumsum(da, axis=1)
    diff = acs[:, :, None] - acs[:, None]
    decay = jnp.exp(jnp.where(causal_mask(l)[None, :, :, None, None], diff, -jnp.inf))
    cb = jnp.einsum("btgn,bsgn->btsg", cm, bm)
    y = jnp.einsum("btsg,btsgh,bsghp->btghp", cb, decay, x)
    y = y + jnp.einsum("btgn,bghpn->btghp", cm, s) * jnp.exp(acs)[..., None]
    tail = jnp.exp(acs[:, -1:] - acs)
    s_new = jnp.exp(acs[:, -1])[..., None, None] * s + jnp.einsum("bsgn,bsghp->bghpn", bm, x * tail[..., None])
    return s_new, y


def ssd_mixer(z, xbc, dt, conv_prev, s0, conv_w, conv_b, dt_bias, a_log, d_skip, norm_w):
    f32 = jnp.float32
    b_, t = z.shape[:2]
    xbc, conv_new = causal_conv(xbc, conv_prev, conv_w, conv_b)
    xbc = jax.nn.silu(xbc.astype(f32))
    xs, bm, cm = jnp.split(xbc, [B_WIDTH, B_WIDTH + B_GROUPS * B_STATE], axis=-1)
    xs = xs.reshape(b_, t, B_GROUPS, B_HPG, B_HEAD_DIM)
    bm = bm.reshape(b_, t, B_GROUPS, B_STATE)
    cm = cm.reshape(b_, t, B_GROUPS, B_STATE)
    dt = jax.nn.softplus(dt.astype(f32) + dt_bias.astype(f32)).reshape(b_, t, B_GROUPS, B_HPG)
    a = -jnp.exp(a_log.astype(f32)).reshape(B_GROUPS, B_HPG)
    cs = chunk_size(t)
    s0 = s0.astype(f32).reshape(b_, B_GROUPS, B_HPG, B_HEAD_DIM, B_STATE)
    inp = (to_chunks(xs * dt[..., None], cs), to_chunks(dt * a, cs), to_chunks(bm, cs), to_chunks(cm, cs))
    s_t, y = lax.scan(ssd_chunk, s0, inp)
    y = from_chunks(y) + xs * d_skip.astype(f32).reshape(B_GROUPS, B_HPG)[:, :, None]
    v = (y.reshape(b_, t, B_WIDTH) * jax.nn.silu(z.astype(f32))).reshape(b_, t, B_GROUPS, B_WIDTH // B_GROUPS)
    v = v * lax.rsqrt(jnp.mean(v * v, axis=-1, keepdims=True) + EPS)
    out = v.reshape(b_, t, B_WIDTH) * norm_w.astype(f32)
    return out, conv_new, s_t.reshape(b_, B_HEADS, B_HEAD_DIM, B_STATE)


def hgrn_chunk(s, inp):
    q, lf, k, v = inp
    l = q.shape[1]
    bc = jnp.cumsum(lf, axis=1)
    diff = bc[:, :, None] - bc[:, None]
    decay = jnp.exp(jnp.where(causal_mask(l)[None, :, :, None, None], diff, -jnp.inf))
    att = jnp.einsum("bthk,bshk,btshk->bths", q, k, decay)
    o = jnp.einsum("bths,bshv->bthv", att, v) + jnp.einsum("bthk,bhkv->bthv", q * jnp.exp(bc), s)
    bl = bc[:, -1]
    s_new = jnp.exp(bl)[..., None] * s + jnp.einsum("bshk,bshv->bhkv", k * jnp.exp(bl[:, None] - bc), v)
    return s_new, o


def hgrn_mixer(q, fpre, ig, og, s0, lb, norm_w):
    f32 = jnp.float32
    b_, t = q.shape[:2]
    shp = (b_, t, C_HEADS, C_KDIM)
    q = jax.nn.silu(q.astype(f32)).reshape(shp)
    fpre = fpre.astype(f32)
    log_f = jnp.logaddexp(jnp.log(lb), jnp.log1p(-lb) + jax.nn.log_sigmoid(fpre)).reshape(shp)
    k = ((1.0 - lb) * jax.nn.sigmoid(-fpre)).reshape(shp)
    v = ig.astype(f32).reshape(b_, t, C_HEADS, C_VDIM)
    cs = chunk_size(t)
    inp = (to_chunks(q, cs), to_chunks(log_f, cs), to_chunks(k, cs), to_chunks(v, cs))
    s_t, o = lax.scan(hgrn_chunk, s0.astype(f32), inp)
    o = from_chunks(o).reshape(b_, t, C_WIDTH)
    o = o * lax.rsqrt(jnp.mean(o * o, axis=-1, keepdims=True) + EPS) * norm_w.astype(f32)
    return o * jax.nn.silu(og.astype(f32)), s_t


def trunk_layer(x, conv_a0, h_a0, conv_b0, s_b0, s_c0, lb, p):
    f32 = jnp.float32
    h = rmsnorm(x, p["norm1_w"])
    proj = jnp.einsum("btd,de->bte", h, p["w_in"])
    xa, ga, z, xbc, dt, q, fpre, ig, og = jnp.split(proj, SPLIT_IDX, axis=-1)
    xa, conv_a = causal_conv(xa, conv_a0, p["rglru_conv_w"], p["rglru_conv_b"])
    ha, h_a = rglru(xa, h_a0, p["rglru_wa"], p["rglru_ba"], p["rglru_wx"], p["rglru_bx"], p["rglru_lambda"])
    ya = ha * jax.nn.gelu(ga.astype(f32))
    yb, conv_b, s_b = ssd_mixer(z, xbc, dt, conv_b0, s_b0, p["ssd_conv_w"], p["ssd_conv_b"],
                                p["ssd_dt_bias"], p["ssd_a_log"], p["ssd_d"], p["ssd_norm_w"])
    yc, s_c = hgrn_mixer(q, fpre, ig, og, s_c0, lb, p["hgrn_norm_w"])
    mix = jnp.concatenate([ya, yb, yc], axis=-1).astype(x.dtype)
    x = x + jnp.einsum("bte,ed->btd", mix, p["w_out"])
    h2 = rmsnorm(x, p["norm2_w"])
    ff = jax.nn.silu(jnp.einsum("btd,df->btf", h2, p["w_ffn_gate"])) * jnp.einsum("btd,df->btf", h2, p["w_ffn_up"])
    x = x + jnp.einsum("btf,fd->btd", ff, p["w_ffn_down"])
    return x, conv_a, h_a, conv_b, s_b, s_c


def setup_inputs(seed: int = 0) -> dict:
    key = jax.random.key(seed)
    ks = iter(jax.random.split(key, 40))
    f32 = jnp.float32

    def nrm(shape, scale):
        return scale * jax.random.normal(next(ks), shape, f32)

    def gain(shape):
        return 1.0 + 0.02 * jax.random.normal(next(ks), shape, f32)

    a0 = jax.random.uniform(next(ks), (DEPTH, A_WIDTH), f32, 0.9, 0.999)
    dt0 = jnp.exp(jax.random.uniform(next(ks), (DEPTH, B_HEADS), f32, math.log(1e-3), math.log(1e-1)))
    a_init = jax.random.uniform(next(ks), (DEPTH, B_HEADS), f32, 1.0, 16.0)
    return {
        "x_prompt": nrm((BATCH, SEQ, D_MODEL), 1.0),
        "x_sample": nrm((DEC_BATCH, DEC_SEQ, D_MODEL), 1.0),
        "state_rglru_conv": nrm((DEPTH, DEC_BATCH, CONV_W - 1, A_WIDTH), 1.0),
        "state_rglru_h": nrm((DEPTH, DEC_BATCH, A_WIDTH), 0.5),
        "state_ssd_conv": nrm((DEPTH, DEC_BATCH, CONV_W - 1, B_CONV_DIM), 1.0),
        "state_ssd": nrm((DEPTH, DEC_BATCH, B_HEADS, B_HEAD_DIM, B_STATE), 0.1),
        "state_hgrn": nrm((DEPTH, DEC_BATCH, C_HEADS, C_KDIM, C_VDIM), 0.5),
        "norm1_w": gain((DEPTH, D_MODEL)),
        "w_in": nrm((DEPTH, D_MODEL, IN_COLS), D_MODEL ** -0.5),
        "rglru_conv_w": nrm((DEPTH, CONV_W, A_WIDTH), CONV_W ** -0.5),
        "rglru_conv_b": nrm((DEPTH, A_WIDTH), 0.02),
        "rglru_wa": nrm((DEPTH, A_HEADS, A_HEAD_DIM, A_HEAD_DIM), A_HEAD_DIM ** -0.5),
        "rglru_ba": nrm((DEPTH, A_WIDTH), 0.02),
        "rglru_wx": nrm((DEPTH, A_HEADS, A_HEAD_DIM, A_HEAD_DIM), A_HEAD_DIM ** -0.5),
        "rglru_bx": nrm((DEPTH, A_WIDTH), 0.02),
        "rglru_lambda": jnp.log(a0) - jnp.log1p(-a0),
        "ssd_conv_w": nrm((DEPTH, CONV_W, B_CONV_DIM), CONV_W ** -0.5),
        "ssd_conv_b": nrm((DEPTH, B_CONV_DIM), 0.02),
        "ssd_dt_bias": dt0 + jnp.log(-jnp.expm1(-dt0)),
        "ssd_a_log": jnp.log(a_init),
        "ssd_d": gain((DEPTH, B_HEADS)),
        "ssd_norm_w": gain((DEPTH, B_WIDTH)),
        "hgrn_lb": nrm((DEPTH, C_WIDTH), 1.0),
        "hgrn_norm_w": gain((DEPTH, C_WIDTH)),
        "w_out": nrm((DEPTH, D_MODEL, D_MODEL), D_MODEL ** -0.5),
        "norm2_w": gain((DEPTH, D_MODEL)),
        "w_ffn_gate": nrm((DEPTH, D_MODEL, D_FF), D_MODEL ** -0.5),
        "w_ffn_up": nrm((DEPTH, D_MODEL, D_FF), D_MODEL ** -0.5),
        "w_ffn_down": nrm((DEPTH, D_FF, D_MODEL), D_FF ** -0.5),
        "final_norm_w": gain((D_MODEL,)),
    }


def reference(x_prompt, x_sample, state_rglru_conv, state_rglru_h, state_ssd_conv, state_ssd, state_hgrn,
              norm1_w, w_in, rglru_conv_w, rglru_conv_b, rglru_wa, rglru_ba, rglru_wx, rglru_bx, rglru_lambda,
              ssd_conv_w, ssd_conv_b, ssd_dt_bias, ssd_a_log, ssd_d, ssd_norm_w, hgrn_lb, hgrn_norm_w,
              w_out, norm2_w, w_ffn_gate, w_ffn_up, w_ffn_down, final_norm_w):
    f32 = jnp.float32
    lbs = jnp.cumsum(jax.nn.softmax(hgrn_lb.astype(f32), axis=0), axis=0)
    lbs = lbs - lbs[0]
    bp = x_prompt.shape[0]
    xp, xs = x_prompt, x_sample
    p_acc = ([], [], [], [], [])
    s_acc = ([], [], [], [], [])
    for l in range(DEPTH):
        p = {
            "norm1_w": norm1_w[l], "w_in": w_in[l],
            "rglru_conv_w": rglru_conv_w[l], "rglru_conv_b": rglru_conv_b[l],
            "rglru_wa": rglru_wa[l], "rglru_ba": rglru_ba[l], "rglru_wx": rglru_wx[l], "rglru_bx": rglru_bx[l],
            "rglru_lambda": rglru_lambda[l],
            "ssd_conv_w": ssd_conv_w[l], "ssd_conv_b": ssd_conv_b[l], "ssd_dt_bias": ssd_dt_bias[l],
            "ssd_a_log": ssd_a_log[l], "ssd_d": ssd_d[l], "ssd_norm_w": ssd_norm_w[l],
            "hgrn_norm_w": hgrn_norm_w[l], "w_out": w_out[l], "norm2_w": norm2_w[l],
            "w_ffn_gate": w_ffn_gate[l], "w_ffn_up": w_ffn_up[l], "w_ffn_down": w_ffn_down[l],
        }
        xp, *new_p = trunk_layer(
            xp,
            jnp.zeros((bp, CONV_W - 1, A_WIDTH), x_prompt.dtype),
            jnp.zeros((bp, A_WIDTH), f32),
            jnp.zeros((bp, CONV_W - 1, B_CONV_DIM), x_prompt.dtype),
            jnp.zeros((bp, B_HEADS, B_HEAD_DIM, B_STATE), f32),
            jnp.zeros((bp, C_HEADS, C_KDIM, C_VDIM), f32),
            lbs[l], p)
        xs, *new_s = trunk_layer(
            xs, state_rglru_conv[l], state_rglru_h[l], state_ssd_conv[l], state_ssd[l], state_hgrn[l],
            lbs[l], p)
        for acc, v in zip(p_acc, new_p):
            acc.append(v)
        for acc, v in zip(s_acc, new_s):
            acc.append(v)
    y_prompt = rmsnorm(xp, final_norm_w)
    y_sample = rmsnorm(xs, final_norm_w)
    p_rglru_conv, p_rglru_h, p_ssd_conv, p_ssd, p_hgrn = (jnp.stack(v) for v in p_acc)
    s_rglru_conv, s_rglru_h, s_ssd_conv, s_ssd, s_hgrn = (jnp.stack(v) for v in s_acc)
    return (y_prompt, y_sample, p_rglru_conv, p_rglru_h, p_ssd_conv, p_ssd, p_hgrn,
            s_rglru_conv, s_rglru_h, s_ssd_conv, s_ssd, s_hgrn)
```

```python
import functools

import jax
import jax.numpy as jnp
from jax import lax
from jax.experimental import pallas as pl
from jax.experimental.pallas import tpu as pltpu

F32 = jnp.float32
BF16 = jnp.bfloat16

D_MODEL = 1024
DEPTH = 2
CHUNK = 64
EPS = 1e-6
CONV_W = 4
A_WIDTH = 256
A_HEADS = 4
A_HEAD_DIM = A_WIDTH // A_HEADS
LRU_C = 8.0
B_WIDTH = 512
B_HEAD_DIM = 64
B_HEADS = B_WIDTH // B_HEAD_DIM
B_GROUPS = 2
B_HPG = B_HEADS // B_GROUPS
B_STATE = 128
B_GROUP_WIDTH = B_WIDTH // B_GROUPS
B_CONV_DIM = B_WIDTH + 2 * B_GROUPS * B_STATE
C_WIDTH = 256
C_HEADS = 4
C_KDIM = C_WIDTH // C_HEADS
D_FF = 2816

HALO = 8
SUB = 16
PROMPT_ROWS = 256
FFN_ROWS = 256
VMEM_LIMIT_BYTES = 56 * 1024 * 1024

OFF_XA = 0
OFF_GA = OFF_XA + A_WIDTH
OFF_Z = OFF_GA + A_WIDTH
OFF_XBC = OFF_Z + B_WIDTH
OFF_Q = OFF_XBC + B_CONV_DIM
OFF_F = OFF_Q + C_WIDTH
OFF_I = OFF_F + C_WIDTH
OFF_G = OFF_I + C_WIDTH
OFF_DT = OFF_G + C_WIDTH
IN_COLS_PADDED = OFF_DT + B_WIDTH


def _dot(a, b):
    return jnp.dot(a.astype(BF16), b.astype(BF16), preferred_element_type=F32)


def _dot_nt(a, b):
    return lax.dot_general(a.astype(BF16), b.astype(BF16), (((1,), (1,)), ((), ())),
                           preferred_element_type=F32)


def _dot_tn(a, b):
    return lax.dot_general(a.astype(BF16), b.astype(BF16), (((0,), (0,)), ((), ())),
                           preferred_element_type=F32)


def _rows(shape):
    return lax.broadcasted_iota(jnp.int32, shape, 0)


def _lanes(shape):
    return lax.broadcasted_iota(jnp.int32, shape, 1)


def _rmsnorm(x, w):
    return x * lax.rsqrt(jnp.mean(x * x, axis=-1, keepdims=True) + EPS) * w


def _shift_rows(x, prev, d):
    r = pltpu.roll(x, d, 0)
    p = pltpu.roll(prev, d, 0)
    head = jnp.where(_rows(prev.shape) < d, p, r[:HALO])
    if x.shape[0] == HALO:
        return head
    return jnp.concatenate([head, r[HALO:]], axis=0)


def _causal_conv(x, prev, w, b):
    out = b + x * w[CONV_W - 1:CONV_W]
    for d in range(1, CONV_W):
        out = out + _shift_rows(x, prev, d) * w[CONV_W - 1 - d:CONV_W - d]
    return out


def _cumsum_rows(x, period):
    pos = _rows(x.shape) % period
    d = 1
    while d < period:
        x = x + jnp.where(pos >= d, pltpu.roll(x, d, 0), 0.0)
        d *= 2
    return x


def _linear_scan(a, u):
    n = a.shape[0]
    row = _rows(a.shape)
    d = 1
    while d < n:
        keep = row >= d
        u = jnp.where(keep, a * pltpu.roll(u, d, 0) + u, u)
        a = jnp.where(keep, a * pltpu.roll(a, d, 0), a)
        d *= 2
    return a, u


def _pad_rows(x, n):
    if x.shape[0] == n:
        return x
    return jnp.concatenate([x, jnp.zeros((n - x.shape[0], x.shape[1]), x.dtype)], axis=0)


def _tile_rows(x, reps):
    return jnp.concatenate([x] * reps, axis=0)


def _ssd_chunk(xdt, da, bm, cm, st, consts):
    cs = xdt.shape[0]
    diag_m, causal_m, blockdiag = consts
    acs = _cumsum_rows(da, cs)
    acs_s = jnp.sum(jnp.where(diag_m, acs, 0.0), axis=0, keepdims=True)
    decay = jnp.exp(jnp.where(causal_m, acs - acs_s, -jnp.inf))
    acs_last = acs[cs - 1:cs]
    xt = xdt * jnp.exp(acs_last - acs)
    y_parts, new_st = [], []
    for g in range(B_GROUPS):
        ns = slice(g * B_STATE, (g + 1) * B_STATE)
        ws = slice(g * B_GROUP_WIDTH, (g + 1) * B_GROUP_WIDTH)
        cb = _dot_nt(cm[:, ns], _tile_rows(_pad_rows(bm[:, ns], CHUNK), B_HPG))
        w = cb * decay[:, ws]
        xbd = jnp.where(blockdiag, _tile_rows(_pad_rows(xdt[:, ws], CHUNK), B_HPG), 0.0)
        y = _dot(w, xbd) + _dot(cm[:, ns], st[g]) * jnp.exp(acs[:, ws])
        y_parts.append(y)
        new_st.append(jnp.exp(acs_last[:, ws]) * st[g] + _dot_tn(bm[:, ns], xt[:, ws]))
    return jnp.concatenate(y_parts, axis=1), new_st


def _hgrn_chunk(q, k, v, bc, st, consts):
    cs = q.shape[0]
    blockdiag, strip_m = consts
    bl = bc[cs - 1:cs]
    o = _dot_nt(q * jnp.exp(bc), st)
    row = _rows(q.shape)
    for j in range(cs // SUB - 1):
        lo, hi = j * SUB, (j + 1) * SUB
        e_j = bc[hi - 1:hi]
        k_j = k[lo:hi] * jnp.exp(e_j - bc[lo:hi])
        q_j = q * jnp.exp(jnp.where(row >= hi, bc - e_j, -jnp.inf))
        kbd = jnp.where(strip_m, _tile_rows(k_j, C_HEADS), 0.0)
        vbd = jnp.where(strip_m, _tile_rows(v[lo:hi], C_HEADS), 0.0)
        o = o + _dot(_dot_nt(q_j, kbd), vbd)
    upd = _dot_tn(v, k * jnp.exp(bl - bc))
    return o, st * jnp.exp(bl) + jnp.where(blockdiag, upd, 0.0)


def _mixer_kernel(layer, nb, ts, cs,
                  x_ref, ca0_ref, ha0_ref, cb0_ref, sb0_ref, sc0_ref,
                  n1w_ref, win_ref, acw_ref, acb_ref, wgate_ref, bgate_ref, lam_ref,
                  bcw_ref, bcb_ref, dtb_ref, alog_ref, dskip_ref, bnw_ref,
                  lb_ref, cnw_ref, wout_ref,
                  xo_ref, cao_ref, hao_ref, cbo_ref, sbo_ref, sco_ref,
                  ca_s, ha_s, cb_s, sb_s, sc_s):
    t = pl.program_id(1)
    rows = nb * ts

    @pl.when(t == 0)
    def _():
        ca_s[...] = ca0_ref[...]
        ha_s[...] = ha0_ref[...]
        cb_s[...] = cb0_ref[...]
        sb_s[...] = sb0_ref[...]
        sc_s[...] = sc0_ref[...]

    x = x_ref[0]
    h = _rmsnorm(x, n1w_ref[...]).astype(BF16)

    def proj(off, width):
        return jnp.dot(h, win_ref[:, off:off + width], preferred_element_type=F32)

    def seg(v, s):
        return v[s * ts:(s + 1) * ts]

    def cat_rows(parts):
        return parts[0] if len(parts) == 1 else jnp.concatenate(parts, axis=0)

    xa = proj(OFF_XA, A_WIDTH)
    ga = proj(OFF_GA, A_WIDTH)
    acw = acw_ref[...]
    xc = cat_rows([_causal_conv(seg(xa, s), ca_s[s], acw, acb_ref[...]) for s in range(nb)])
    for s in range(nb):
        ca_s[s] = seg(xa, s)[ts - HALO:]
    gates = _dot(xc, wgate_ref[...]) + bgate_ref[...]
    r_gate = jax.nn.sigmoid(gates[:, :A_WIDTH])
    i_gate = jax.nn.sigmoid(gates[:, A_WIDTH:])
    log_a = -LRU_C * r_gate * jax.nn.softplus(-lam_ref[...])
    a = jnp.exp(log_a)
    u = jnp.sqrt(-jnp.tanh(log_a) * (a * a + 1.0)) * (i_gate * xc)
    ha_parts = []
    for s in range(nb):
        a_run, h_zero = _linear_scan(seg(a, s), seg(u, s))
        h_seq = h_zero + a_run * ha_s[s]
        ha_s[s] = h_seq[ts - 1:]
        ha_parts.append(h_seq)
    ya = cat_rows(ha_parts) * jax.nn.gelu(ga)

    z = proj(OFF_Z, B_WIDTH)
    xbc = proj(OFF_XBC, B_CONV_DIM)
    dt = jax.nn.softplus(proj(OFF_DT, B_WIDTH) + dtb_ref[...])
    bcw = bcw_ref[...]
    xbc_c = cat_rows([_causal_conv(seg(xbc, s), cb_s[s], bcw, bcb_ref[...]) for s in range(nb)])
    for s in range(nb):
        cb_s[s] = seg(xbc, s)[ts - HALO:]
    xbc_c = jax.nn.silu(xbc_c)
    xs = xbc_c[:, :B_WIDTH]
    bm = xbc_c[:, B_WIDTH:B_WIDTH + B_GROUPS * B_STATE]
    cm = xbc_c[:, B_WIDTH + B_GROUPS * B_STATE:]
    da = dt * (-jnp.exp(alog_ref[...]))
    xdt = xs * dt
    pos512 = _lanes((cs, B_WIDTH)) % B_HEAD_DIM
    row512 = _rows((cs, B_WIDTH))
    sq = (B_GROUP_WIDTH, B_GROUP_WIDTH)
    blockdiag = (_rows(sq) // B_HEAD_DIM) == (_lanes(sq) // B_HEAD_DIM)
    ssd_consts = (row512 == pos512, row512 >= pos512, blockdiag)
    yb_parts = []
    for s in range(nb):
        st = [sb_s[s, g] for g in range(B_GROUPS)]
        for c in range(ts // cs):
            sl = slice(s * ts + c * cs, s * ts + (c + 1) * cs)
            y_c, st = _ssd_chunk(xdt[sl], da[sl], bm[sl], cm[sl], st, ssd_consts)
            yb_parts.append(y_c)
        for g in range(B_GROUPS):
            sb_s[s, g] = st[g]
    yb = cat_rows(yb_parts) + xs * dskip_ref[...]
    vb = yb * jax.nn.silu(z)
    vb2 = vb * vb
    inv = [jnp.broadcast_to(
        lax.rsqrt(jnp.mean(vb2[:, g * B_GROUP_WIDTH:(g + 1) * B_GROUP_WIDTH], axis=-1, keepdims=True) + EPS),
        (rows, B_GROUP_WIDTH)) for g in range(B_GROUPS)]
    yb = vb * jnp.concatenate(inv, axis=1) * bnw_ref[...]

    lb_all = lb_ref[...]
    lb_e = jnp.exp(lb_all - jnp.max(lb_all, axis=0, keepdims=True))
    lb_sm = lb_e / jnp.sum(lb_e, axis=0, keepdims=True)
    lb_cum = lb_sm[0:1]
    lb_first = lb_cum
    for d in range(1, layer + 1):
        lb_cum = lb_cum + lb_sm[d:d + 1]
    lb = lb_cum - lb_first
    q = jax.nn.silu(proj(OFF_Q, C_WIDTH))
    fpre = proj(OFF_F, C_WIDTH)
    v = proj(OFF_I, C_WIDTH)
    og = proj(OFF_G, C_WIDTH)
    log_f = jnp.logaddexp(jnp.log(lb), jnp.log1p(-lb) + jax.nn.log_sigmoid(fpre))
    k = (1.0 - lb) * jax.nn.sigmoid(-fpre)
    bc = _cumsum_rows(log_f, cs)
    strip_shape = (C_HEADS * SUB, C_WIDTH)
    strip_m = (_rows(strip_shape) // SUB) == (_lanes(strip_shape) // C_KDIM)
    hgrn_consts = (blockdiag, strip_m)
    o_parts = []
    for s in range(nb):
        st = sc_s[s]
        for c in range(ts // cs):
            sl = slice(s * ts + c * cs, s * ts + (c + 1) * cs)
            o_c, st = _hgrn_chunk(q[sl], k[sl], v[sl], bc[sl], st, hgrn_consts)
            o_parts.append(o_c)
        sc_s[s] = st
    o = cat_rows(o_parts)
    head_ones = jnp.where(blockdiag, 1.0, 0.0).astype(BF16)
    pos = _rows((rows, C_WIDTH)) % SUB
    for d in range(SUB):
        if d == 0:
            p = q * k
            v_d = v
        else:
            live = pos >= d
            w = jnp.exp(jnp.where(live, bc - pltpu.roll(bc, d, 0), -jnp.inf))
            p = q * pltpu.roll(k, d, 0) * w
            v_d = pltpu.roll(v, d, 0)
        o = o + _dot(p, head_ones) * v_d
    yc = o * lax.rsqrt(jnp.mean(o * o, axis=-1, keepdims=True) + EPS) * cnw_ref[...] * jax.nn.silu(og)

    mix = jnp.concatenate([ya, yb, yc], axis=1)
    xo_ref[0] = x + _dot(mix, wout_ref[...])

    @pl.when(t == pl.num_programs(1) - 1)
    def _():
        cao_ref[...] = ca_s[...]
        hao_ref[...] = ha_s[...]
        cbo_ref[...] = cb_s[...]
        sbo_ref[...] = sb_s[...]
        sco_ref[...] = sc_s[...]


def _ffn_kernel(final, x_ref, n2w_ref, wg_ref, wu_ref, wd_ref, fw_ref, o_ref):
    x = x_ref[...]
    h = _rmsnorm(x, n2w_ref[...]).astype(BF16)
    gate = jnp.dot(h, wg_ref[...], preferred_element_type=F32)
    up = jnp.dot(h, wu_ref[...], preferred_element_type=F32)
    y = x + _dot(jax.nn.silu(gate) * up, wd_ref[...])
    if final:
        y = _rmsnorm(y, fw_ref[...])
    o_ref[...] = y


def _const_spec(shape):
    zeros = (0,) * len(shape)
    return pl.BlockSpec(shape, lambda *_: zeros)


def _mixer_call(layer, x, states, weights, nb, ts, cs):
    groups, group_rows, _ = x.shape
    n_steps = group_rows // (nb * ts)
    rows = nb * ts
    x_spec = pl.BlockSpec((1, rows, D_MODEL), lambda g, t: (g, t, 0))

    def state_spec(a):
        shape = (nb,) + a.shape[1:]
        tail = (0,) * (a.ndim - 1)
        return pl.BlockSpec(shape, lambda g, t: (g,) + tail)

    state_specs = [state_spec(a) for a in states]
    state_shapes = [jax.ShapeDtypeStruct(a.shape, F32) for a in states]
    scratch = [pltpu.VMEM((nb,) + a.shape[1:], F32) for a in states]
    return pl.pallas_call(
        functools.partial(_mixer_kernel, layer, nb, ts, cs),
        grid=(groups, n_steps),
        in_specs=[x_spec] + state_specs + [_const_spec(w.shape) for w in weights],
        out_specs=[x_spec] + state_specs,
        out_shape=[jax.ShapeDtypeStruct(x.shape, F32)] + state_shapes,
        scratch_shapes=scratch,
        compiler_params=pltpu.CompilerParams(
            dimension_semantics=("arbitrary", "arbitrary"), vmem_limit_bytes=VMEM_LIMIT_BYTES),
        name=f"mixer_l{layer}_nb{nb}",
    )(x, *states, *weights)


def _ffn_call(final, x, weights, block_rows, tag):
    n_rows = x.shape[0]
    x_spec = pl.BlockSpec((block_rows, D_MODEL), lambda i: (i, 0))
    return pl.pallas_call(
        functools.partial(_ffn_kernel, final),
        grid=(n_rows // block_rows,),
        in_specs=[x_spec] + [_const_spec(w.shape) for w in weights],
        out_specs=x_spec,
        out_shape=jax.ShapeDtypeStruct(x.shape, F32),
        compiler_params=pltpu.CompilerParams(
            dimension_semantics=("arbitrary",), vmem_limit_bytes=VMEM_LIMIT_BYTES),
        name=f"ffn_{tag}",
    )(x, *weights)


def _block_diag(w):
    heads, n_i, n_j = w.shape
    eye = jnp.eye(heads, dtype=w.dtype)
    return jnp.einsum("hij,hg->higj", w, eye).reshape(heads * n_i, heads * n_j)


def _ssd_state_to_kernel(s):
    b = s.shape[0]
    s = s.reshape(b, B_GROUPS, B_HPG, B_HEAD_DIM, B_STATE)
    return s.transpose(0, 1, 4, 2, 3).reshape(b, B_GROUPS, B_STATE, B_GROUP_WIDTH)


def _ssd_state_from_kernel(s):
    b = s.shape[0]
    s = s.reshape(b, B_GROUPS, B_STATE, B_HPG, B_HEAD_DIM)
    return s.transpose(0, 1, 3, 4, 2).reshape(b, B_HEADS, B_HEAD_DIM, B_STATE)


def _hgrn_state_to_kernel(s):
    b = s.shape[0]
    eye = jnp.eye(C_HEADS, dtype=s.dtype)
    return jnp.einsum("bhkv,hg->bhvgk", s, eye).reshape(b, C_WIDTH, C_WIDTH)


def _hgrn_state_from_kernel(s):
    b = s.shape[0]
    return jnp.einsum("bhvhk->bhkv", s.reshape(b, C_HEADS, C_KDIM, C_HEADS, C_KDIM))


def _conv_state_to_kernel(s):
    return jnp.pad(s, ((0, 0), (HALO - (CONV_W - 1), 0), (0, 0)))


def _row(v):
    return v.reshape(1, -1).astype(F32)


def kernel(x_prompt, x_sample, state_rglru_conv, state_rglru_h, state_ssd_conv, state_ssd, state_hgrn,
           norm1_w, w_in, rglru_conv_w, rglru_conv_b, rglru_wa, rglru_ba, rglru_wx, rglru_bx, rglru_lambda,
           ssd_conv_w, ssd_conv_b, ssd_dt_bias, ssd_a_log, ssd_d, ssd_norm_w, hgrn_lb, hgrn_norm_w,
           w_out, norm2_w, w_ffn_gate, w_ffn_up, w_ffn_down, final_norm_w):
    bp, seq, _ = x_prompt.shape
    bs, dseq, _ = x_sample.shape
    assert seq % PROMPT_ROWS == 0 and PROMPT_ROWS % CHUNK == 0
    assert dseq <= CHUNK and dseq % SUB == 0 and dseq >= HALO
    split = [0, A_WIDTH, 2 * A_WIDTH, 2 * A_WIDTH + B_WIDTH, 2 * A_WIDTH + B_WIDTH + B_CONV_DIM]
    split.append(split[-1] + B_HEADS)
    for _ in range(4):
        split.append(split[-1] + C_WIDTH)

    def expand_heads(v):
        return _row(jnp.repeat(v, B_HEAD_DIM))

    xp = x_prompt
    xs = x_sample.reshape(1, bs * dseq, D_MODEL)
    p_out = ([], [], [], [], [])
    s_out = ([], [], [], [], [])
    y_prompt = y_sample = None
    for l in range(DEPTH):
        w = w_in[l]
        pieces = [w[:, split[i]:split[i + 1]] for i in range(9)]
        xa_w, ga_w, z_w, xbc_w, dt_w, q_w, f_w, i_w, g_w = pieces
        win = jnp.concatenate([xa_w, ga_w, z_w, xbc_w, q_w, f_w, i_w, g_w,
                               jnp.repeat(dt_w, B_HEAD_DIM, axis=1)], axis=1).astype(BF16)
        mixer_weights = [
            _row(norm1_w[l]), win,
            rglru_conv_w[l], _row(rglru_conv_b[l]),
            jnp.concatenate([_block_diag(rglru_wa[l]), _block_diag(rglru_wx[l])], axis=1).astype(BF16),
            _row(jnp.concatenate([rglru_ba[l], rglru_bx[l]])), _row(rglru_lambda[l]),
            ssd_conv_w[l], _row(ssd_conv_b[l]), expand_heads(ssd_dt_bias[l]), expand_heads(ssd_a_log[l]),
            expand_heads(ssd_d[l]), _row(ssd_norm_w[l]),
            hgrn_lb.astype(F32), _row(hgrn_norm_w[l]), w_out[l].astype(BF16),
        ]
        ffn_weights = [_row(norm2_w[l]), w_ffn_gate[l].astype(BF16), w_ffn_up[l].astype(BF16),
                       w_ffn_down[l].astype(BF16), _row(final_norm_w)]
        final = l == DEPTH - 1

        p_states = [jnp.zeros((bp, HALO, A_WIDTH), F32), jnp.zeros((bp, 1, A_WIDTH), F32),
                    jnp.zeros((bp, HALO, B_CONV_DIM), F32),
                    jnp.zeros((bp, B_GROUPS, B_STATE, B_GROUP_WIDTH), F32),
                    jnp.zeros((bp, C_WIDTH, C_WIDTH), F32)]
        xp, *new_p = _mixer_call(l, xp, p_states, mixer_weights, 1, PROMPT_ROWS, CHUNK)
        xp = _ffn_call(final, xp.reshape(bp * seq, D_MODEL), ffn_weights, FFN_ROWS, f"prompt_l{l}")
        xp = xp.reshape(bp, seq, D_MODEL)

        s_states = [_conv_state_to_kernel(state_rglru_conv[l]), state_rglru_h[l].reshape(bs, 1, A_WIDTH),
                    _conv_state_to_kernel(state_ssd_conv[l]), _ssd_state_to_kernel(state_ssd[l]),
                    _hgrn_state_to_kernel(state_hgrn[l])]
        xs, *new_s = _mixer_call(l, xs, s_states, mixer_weights, bs, dseq, dseq)
        xs = _ffn_call(final, xs.reshape(bs * dseq, D_MODEL), ffn_weights, bs * dseq, f"sample_l{l}")
        xs = xs.reshape(1, bs * dseq, D_MODEL)

        for acc, new in ((p_out, new_p), (s_out, new_s)):
            conv_a, h_a, conv_b, s_b, s_c = new
            acc[0].append(conv_a[:, HALO - (CONV_W - 1):])
            acc[1].append(h_a.reshape(-1, A_WIDTH))
            acc[2].append(conv_b[:, HALO - (CONV_W - 1):])
            acc[3].append(_ssd_state_from_kernel(s_b))
            acc[4].append(_hgrn_state_from_kernel(s_c))
    y_prompt = xp
    y_sample = xs.reshape(bs, dseq, D_MODEL)
    return (y_prompt, y_sample, *(jnp.stack(v) for v in p_out), *(jnp.stack(v) for v in s_out))
```

```python
import functools

import jax
import jax.numpy as jnp
from jax import lax
from jax.experimental import pallas as pl
from jax.experimental.pallas import tpu as pltpu

F32 = jnp.float32
BF16 = jnp.bfloat16

D_MODEL = 1024
DEPTH = 2
CHUNK = 64
EPS = 1e-6
CONV_W = 4
A_WIDTH = 256
A_HEADS = 4
A_HEAD_DIM = A_WIDTH // A_HEADS
LRU_C = 8.0
B_WIDTH = 512
B_HEAD_DIM = 64
B_HEADS = B_WIDTH // B_HEAD_DIM
B_GROUPS = 2
B_HPG = B_HEADS // B_GROUPS
B_STATE = 128
B_GROUP_WIDTH = B_WIDTH // B_GROUPS
B_CONV_DIM = B_WIDTH + 2 * B_GROUPS * B_STATE
C_WIDTH = 256
C_HEADS = 4
C_KDIM = C_WIDTH // C_HEADS
D_FF = 2816

SUBLANES = 8
LANES = 128
HALO = SUBLANES
SUB = SUBLANES
PROMPT_ROWS = 256
FFN_ROWS = 512
VMEM_LIMIT_BYTES = 56 * 1024 * 1024

OFF_XA = 0
OFF_GA = OFF_XA + A_WIDTH
OFF_Z = OFF_GA + A_WIDTH
OFF_XBC = OFF_Z + B_WIDTH
OFF_Q = OFF_XBC + B_CONV_DIM
OFF_F = OFF_Q + C_WIDTH
OFF_I = OFF_F + C_WIDTH
OFF_G = OFF_I + C_WIDTH
OFF_DT = OFF_G + C_WIDTH
IN_COLS_PADDED = OFF_DT + LANES


def _dot(a, b):
    return jnp.dot(a.astype(BF16), b.astype(BF16), preferred_element_type=F32)


def _dot_nt(a, b):
    return lax.dot_general(a.astype(BF16), b.astype(BF16), (((1,), (1,)), ((), ())),
                           preferred_element_type=F32)


def _dot_tn(a, b):
    return lax.dot_general(a.astype(BF16), b.astype(BF16), (((0,), (0,)), ((), ())),
                           preferred_element_type=F32)


def _rows(shape):
    return lax.broadcasted_iota(jnp.int32, shape, 0)


def _lanes(shape):
    return lax.broadcasted_iota(jnp.int32, shape, 1)


def _rmsnorm(x, w):
    return x * lax.rsqrt(jnp.mean(x * x, axis=-1, keepdims=True) + EPS) * w


def _cat_rows(parts):
    return parts[0] if len(parts) == 1 else jnp.concatenate(parts, axis=0)


def _shift_rows(x, prev, d):
    r = pltpu.roll(x, d, 0)
    p = pltpu.roll(prev, d, 0)
    head = jnp.where(_rows(prev.shape) < d, p, r[:HALO])
    if x.shape[0] == HALO:
        return head
    return jnp.concatenate([head, r[HALO:]], axis=0)


def _causal_conv(x, prev, w, b):
    out = b + x * w[CONV_W - 1:CONV_W]
    for d in range(1, CONV_W):
        out = out + _shift_rows(x, prev, d) * w[CONV_W - 1 - d:CONV_W - d]
    return out


def _tiles(x):
    n, c = x.shape
    return x.reshape(n // SUBLANES, SUBLANES, c)


def _split_bf16(x, terms):
    pieces = []
    rest = x
    for i in range(terms):
        piece = rest.astype(BF16)
        if i + 1 < terms:
            rest = rest - piece.astype(F32)
        pieces.append(piece)
    return pieces


def _cumsum_rows(tri, x):
    out = None
    for piece in _split_bf16(x, 3):
        part = jnp.dot(tri, piece, preferred_element_type=F32)
        out = part if out is None else out + part
    return out


def _linear_scan(a, u, h0):
    a3 = _tiles(a)
    u3 = _tiles(u)
    pos3 = lax.broadcasted_iota(jnp.int32, a3.shape, 1)
    d = 1
    while d < SUBLANES:
        keep = pos3 >= d
        u3 = jnp.where(keep, a3 * pltpu.roll(u3, d, 1) + u3, u3)
        a3 = jnp.where(keep, a3 * pltpu.roll(a3, d, 1), a3)
        d *= 2
    carry = h0
    tiles = []
    for j in range(a3.shape[0]):
        h_j = u3[j] + a3[j] * carry
        carry = h_j[SUBLANES - 1:]
        tiles.append(h_j)
    return jnp.concatenate(tiles, axis=0)


def _pad_rows(x, n):
    if x.shape[0] == n:
        return x
    return jnp.concatenate([x, jnp.zeros((n - x.shape[0], x.shape[1]), x.dtype)], axis=0)


def _tile_rows(x, reps):
    return jnp.concatenate([x] * reps, axis=0)


def _expand_heads(x, spread, terms):
    out = None
    for piece in _split_bf16(x, terms):
        part = jnp.dot(piece, spread, preferred_element_type=F32)
        out = part if out is None else out + part
    return out


def _ssd_chunk(xdt, acs, bm, cm, st, consts):
    cs = xdt.shape[0]
    diag_m, causal_m, blockdiag = consts
    acs_s = jnp.sum(jnp.where(diag_m, acs, 0.0), axis=0, keepdims=True)
    decay = jnp.exp(jnp.where(causal_m, acs - acs_s, -jnp.inf))
    acs_last = acs[cs - 1:cs]
    xt = xdt * jnp.exp(acs_last - acs)
    y_parts, new_st = [], []
    for g in range(B_GROUPS):
        ns = slice(g * B_STATE, (g + 1) * B_STATE)
        ws = slice(g * B_GROUP_WIDTH, (g + 1) * B_GROUP_WIDTH)
        cb = _dot_nt(cm[:, ns], _tile_rows(_pad_rows(bm[:, ns], CHUNK), B_HPG))
        w = cb * decay[:, ws]
        xbd = jnp.where(blockdiag, _tile_rows(_pad_rows(xdt[:, ws], CHUNK), B_HPG), 0.0)
        y = _dot(w, xbd) + _dot(cm[:, ns], st[g]) * jnp.exp(acs[:, ws])
        y_parts.append(y)
        new_st.append(jnp.exp(acs_last[:, ws]) * st[g] + _dot_tn(bm[:, ns], xt[:, ws]))
    return jnp.concatenate(y_parts, axis=1), new_st


def _hgrn_chunk(q, k, v, bc, st, blockdiag):
    cs = q.shape[0]
    bl = bc[cs - 1:cs]
    o = _dot_nt(q * jnp.exp(bc), st)
    pieces = [o[i * SUB:(i + 1) * SUB] for i in range(cs // SUB)]
    half = cs // 2
    m = SUB
    while m <= half:
        n_blk = cs // (2 * m)
        q_rows, k_rows, v_rows = [], [], []
        for b in range(n_blk):
            lo, mid, hi = 2 * m * b, 2 * m * b + m, 2 * m * (b + 1)
            e = bc[mid - 1:mid]
            k_rows.append(k[lo:mid] * jnp.exp(e - bc[lo:mid]))
            v_rows.append(v[lo:mid])
            q_rows.append(q[mid:hi] * jnp.exp(bc[mid:hi] - e))
        q_m, k_m, v_m = _cat_rows(q_rows), _cat_rows(k_rows), _cat_rows(v_rows)
        shape = (C_HEADS * half, C_WIDTH)
        head_m = (_rows(shape) // half) == (_lanes(shape) // C_KDIM)
        kbd = jnp.where(head_m, _tile_rows(k_m, C_HEADS), 0.0)
        vbd = jnp.where(head_m, _tile_rows(v_m, C_HEADS), 0.0)
        att = _dot_nt(q_m, kbd)
        if n_blk > 1:
            same = (_rows(att.shape) // m) == ((_lanes(att.shape) % half) // m)
            att = jnp.where(same, att, 0.0)
        o_m = _dot(att, vbd)
        for b in range(n_blk):
            for i in range(m // SUB):
                dst = (2 * m * b + m) // SUB + i
                src = b * m + i * SUB
                pieces[dst] = pieces[dst] + o_m[src:src + SUB]
        m *= 2
    upd = _dot_tn(v, k * jnp.exp(bl - bc))
    return pieces, st * jnp.exp(bl) + jnp.where(blockdiag, upd, 0.0)


def _mixer_kernel(layer, nb, ts, cs,
                  x_ref, ca0_ref, ha0_ref, cb0_ref, sb0_ref, sc0_ref,
                  n1w_ref, win_ref, acw_ref, acb_ref, wgate_ref, bgate_ref, lam_ref,
                  bcw_ref, bcb_ref, dtb_ref, alog_ref, dskip_ref, bnw_ref,
                  lb_ref, cnw_ref, wout_ref, tri_ref, spread_ref, ones_ref,
                  xo_ref, cao_ref, hao_ref, cbo_ref, sbo_ref, sco_ref,
                  ca_s, ha_s, cb_s, sb_s, sc_s):
    t = pl.program_id(1)
    rows = nb * ts

    @pl.when(t == 0)
    def _():
        ca_s[...] = ca0_ref[...]
        ha_s[...] = ha0_ref[...]
        cb_s[...] = cb0_ref[...]
        sb_s[...] = sb0_ref[...]
        sc_s[...] = sc0_ref[...]

    x = x_ref[0]
    h = _rmsnorm(x, n1w_ref[...]).astype(BF16)

    def proj(off, width):
        return jnp.dot(h, win_ref[:, off:off + width], preferred_element_type=F32)

    def seg(v, s):
        return v[s * ts:(s + 1) * ts]

    xa = proj(OFF_XA, A_WIDTH)
    ga = proj(OFF_GA, A_WIDTH)
    acw = acw_ref[...]
    xc = _cat_rows([_causal_conv(seg(xa, s), ca_s[s], acw, acb_ref[...]) for s in range(nb)])
    for s in range(nb):
        ca_s[s] = seg(xa, s)[ts - HALO:]
    gates = _dot(xc, wgate_ref[...]) + bgate_ref[...]
    r_gate = jax.nn.sigmoid(gates[:, :A_WIDTH])
    i_gate = jax.nn.sigmoid(gates[:, A_WIDTH:])
    log_a = -LRU_C * r_gate * jax.nn.softplus(-lam_ref[...])
    a = jnp.exp(log_a)
    u = jnp.sqrt(-jnp.tanh(log_a) * (a * a + 1.0)) * (i_gate * xc)
    ha_parts = []
    for s in range(nb):
        h_seq = _linear_scan(seg(a, s), seg(u, s), ha_s[s])
        ha_s[s] = h_seq[ts - 1:]
        ha_parts.append(h_seq)
    ya = _cat_rows(ha_parts) * jax.nn.gelu(ga)

    z = proj(OFF_Z, B_WIDTH)
    xbc = proj(OFF_XBC, B_CONV_DIM)
    bcw = bcw_ref[...]
    xbc_c = _cat_rows([_causal_conv(seg(xbc, s), cb_s[s], bcw, bcb_ref[...]) for s in range(nb)])
    for s in range(nb):
        cb_s[s] = seg(xbc, s)[ts - HALO:]
    xbc_c = jax.nn.silu(xbc_c)
    xs = xbc_c[:, :B_WIDTH]
    bm = xbc_c[:, B_WIDTH:B_WIDTH + B_GROUPS * B_STATE]
    cm = xbc_c[:, B_WIDTH + B_GROUPS * B_STATE:]
    dt_c = jax.nn.softplus(proj(OFF_DT, LANES) + dtb_ref[...])
    acs_c = _cumsum_rows(tri_ref[...], dt_c * (-jnp.exp(alog_ref[...])))
    xdt = xs * _expand_heads(dt_c, spread_ref[...], 2)
    acs = _expand_heads(acs_c, spread_ref[...], 3)
    pos512 = _lanes((cs, B_WIDTH)) % B_HEAD_DIM
    row512 = _rows((cs, B_WIDTH))
    sq = (B_GROUP_WIDTH, B_GROUP_WIDTH)
    blockdiag = (_rows(sq) // B_HEAD_DIM) == (_lanes(sq) // B_HEAD_DIM)
    ssd_consts = (row512 == pos512, row512 >= pos512, blockdiag)
    yb_parts = []
    for s in range(nb):
        st = [sb_s[s, g] for g in range(B_GROUPS)]
        for c in range(ts // cs):
            sl = slice(s * ts + c * cs, s * ts + (c + 1) * cs)
            y_c, st = _ssd_chunk(xdt[sl], acs[sl], bm[sl], cm[sl], st, ssd_consts)
            yb_parts.append(y_c)
        for g in range(B_GROUPS):
            sb_s[s, g] = st[g]
    yb = _cat_rows(yb_parts) + xs * dskip_ref[...]
    vb = yb * jax.nn.silu(z)
    vb2 = vb * vb
    inv = [jnp.broadcast_to(
        lax.rsqrt(jnp.mean(vb2[:, g * B_GROUP_WIDTH:(g + 1) * B_GROUP_WIDTH], axis=-1, keepdims=True) + EPS),
        (rows, B_GROUP_WIDTH)) for g in range(B_GROUPS)]
    yb = vb * jnp.concatenate(inv, axis=1) * bnw_ref[...]

    lb_all = lb_ref[...]
    lb_e = jnp.exp(lb_all - jnp.max(lb_all, axis=0, keepdims=True))
    lb_sm = lb_e / jnp.sum(lb_e, axis=0, keepdims=True)
    lb_cum = lb_sm[0:1]
    lb_first = lb_cum
    for d in range(1, layer + 1):
        lb_cum = lb_cum + lb_sm[d:d + 1]
    lb = lb_cum - lb_first
    q = jax.nn.silu(proj(OFF_Q, C_WIDTH))
    fpre = proj(OFF_F, C_WIDTH)
    v = proj(OFF_I, C_WIDTH)
    og = proj(OFF_G, C_WIDTH)
    log_f = jnp.logaddexp(jnp.log(lb), jnp.log1p(-lb) + jax.nn.log_sigmoid(fpre))
    k = (1.0 - lb) * jax.nn.sigmoid(-fpre)
    bc = _cumsum_rows(tri_ref[...], log_f)
    o_parts = []
    for s in range(nb):
        st = sc_s[s]
        for c in range(ts // cs):
            sl = slice(s * ts + c * cs, s * ts + (c + 1) * cs)
            pieces, st = _hgrn_chunk(q[sl], k[sl], v[sl], bc[sl], st, blockdiag)
            o_parts.extend(pieces)
        sc_s[s] = st
    o3 = _tiles(_cat_rows(o_parts))
    head_ones = ones_ref[...]
    q3, k3, v3, bc3 = _tiles(q), _tiles(k), _tiles(v), _tiles(bc)
    pos3 = lax.broadcasted_iota(jnp.int32, q3.shape, 1)
    for d in range(SUB):
        if d == 0:
            p3 = q3 * k3
            v_d = v3
        else:
            w = jnp.exp(jnp.where(pos3 >= d, bc3 - pltpu.roll(bc3, d, 1), -jnp.inf))
            p3 = q3 * pltpu.roll(k3, d, 1) * w
            v_d = pltpu.roll(v3, d, 1)
        o3 = o3 + _tiles(_dot(p3.reshape(rows, C_WIDTH), head_ones)) * v_d
    o = o3.reshape(rows, C_WIDTH)
    yc = o * lax.rsqrt(jnp.mean(o * o, axis=-1, keepdims=True) + EPS) * cnw_ref[...] * jax.nn.silu(og)

    mix = jnp.concatenate([ya, yb, yc], axis=1)
    xo_ref[0] = x + _dot(mix, wout_ref[...])

    @pl.when(t == pl.num_programs(1) - 1)
    def _():
        cao_ref[...] = ca_s[...]
        hao_ref[...] = ha_s[...]
        cbo_ref[...] = cb_s[...]
        sbo_ref[...] = sb_s[...]
        sco_ref[...] = sc_s[...]


def _ffn_kernel(final, x_ref, n2w_ref, wg_ref, wu_ref, wd_ref, fw_ref, o_ref):
    x = x_ref[...]
    h = _rmsnorm(x, n2w_ref[...]).astype(BF16)
    gate = jnp.dot(h, wg_ref[...], preferred_element_type=F32)
    up = jnp.dot(h, wu_ref[...], preferred_element_type=F32)
    y = x + _dot(jax.nn.silu(gate) * up, wd_ref[...])
    if final:
        y = _rmsnorm(y, fw_ref[...])
    o_ref[...] = y


def _const_spec(shape):
    zeros = (0,) * len(shape)
    return pl.BlockSpec(shape, lambda *_: zeros, pipeline_mode=pl.Buffered(1))


def _mixer_call(layer, x, states, weights, nb, ts, cs):
    groups, group_rows, _ = x.shape
    n_steps = group_rows // (nb * ts)
    rows = nb * ts
    x_spec = pl.BlockSpec((1, rows, D_MODEL), lambda g, t: (g, t, 0))

    def state_spec(a):
        shape = (nb,) + a.shape[1:]
        tail = (0,) * (a.ndim - 1)
        return pl.BlockSpec(shape, lambda g, t: (g,) + tail)

    state_specs = [state_spec(a) for a in states]
    state_shapes = [jax.ShapeDtypeStruct(a.shape, F32) for a in states]
    scratch = [pltpu.VMEM((nb,) + a.shape[1:], F32) for a in states]
    return pl.pallas_call(
        functools.partial(_mixer_kernel, layer, nb, ts, cs),
        grid=(groups, n_steps),
        in_specs=[x_spec] + state_specs + [_const_spec(w.shape) for w in weights],
        out_specs=[x_spec] + state_specs,
        out_shape=[jax.ShapeDtypeStruct(x.shape, F32)] + state_shapes,
        scratch_shapes=scratch,
        compiler_params=pltpu.CompilerParams(
            dimension_semantics=("arbitrary", "arbitrary"), vmem_limit_bytes=VMEM_LIMIT_BYTES),
        name=f"mixer_l{layer}_nb{nb}",
    )(x, *states, *weights)


def _ffn_call(final, x, weights, block_rows, tag):
    n_rows = x.shape[0]
    x_spec = pl.BlockSpec((block_rows, D_MODEL), lambda i: (i, 0))
    return pl.pallas_call(
        functools.partial(_ffn_kernel, final),
        grid=(n_rows // block_rows,),
        in_specs=[x_spec] + [_const_spec(w.shape) for w in weights],
        out_specs=x_spec,
        out_shape=jax.ShapeDtypeStruct(x.shape, F32),
        compiler_params=pltpu.CompilerParams(
            dimension_semantics=("arbitrary",), vmem_limit_bytes=VMEM_LIMIT_BYTES),
        name=f"ffn_{tag}",
    )(x, *weights)


def _block_diag(w):
    heads, n_i, n_j = w.shape
    eye = jnp.eye(heads, dtype=w.dtype)
    return jnp.einsum("hij,hg->higj", w, eye).reshape(heads * n_i, heads * n_j)


def _ssd_state_to_kernel(s):
    b = s.shape[0]
    s = s.reshape(b, B_GROUPS, B_HPG, B_HEAD_DIM, B_STATE)
    return s.transpose(0, 1, 4, 2, 3).reshape(b, B_GROUPS, B_STATE, B_GROUP_WIDTH)


def _ssd_state_from_kernel(s):
    b = s.shape[0]
    s = s.reshape(b, B_GROUPS, B_STATE, B_HPG, B_HEAD_DIM)
    return s.transpose(0, 1, 3, 4, 2).reshape(b, B_HEADS, B_HEAD_DIM, B_STATE)


def _hgrn_state_to_kernel(s):
    b = s.shape[0]
    eye = jnp.eye(C_HEADS, dtype=s.dtype)
    return jnp.einsum("bhkv,hg->bhvgk", s, eye).reshape(b, C_WIDTH, C_WIDTH)


def _hgrn_state_from_kernel(s):
    b = s.shape[0]
    return jnp.einsum("bhvhk->bhkv", s.reshape(b, C_HEADS, C_KDIM, C_HEADS, C_KDIM))


def _conv_state_to_kernel(s):
    return jnp.pad(s, ((0, 0), (HALO - (CONV_W - 1), 0), (0, 0)))


def _row(v):
    return v.reshape(1, -1).astype(F32)


def _chunk_tri(rows, chunk):
    t = jnp.arange(rows)[:, None]
    s = jnp.arange(rows)[None, :]
    return ((s <= t) & (s // chunk == t // chunk)).astype(BF16)


def _head_row(v):
    return jnp.pad(_row(v), ((0, 0), (0, LANES - B_HEADS)))


def kernel(x_prompt, x_sample, state_rglru_conv, state_rglru_h, state_ssd_conv, state_ssd, state_hgrn,
           norm1_w, w_in, rglru_conv_w, rglru_conv_b, rglru_wa, rglru_ba, rglru_wx, rglru_bx, rglru_lambda,
           ssd_conv_w, ssd_conv_b, ssd_dt_bias, ssd_a_log, ssd_d, ssd_norm_w, hgrn_lb, hgrn_norm_w,
           w_out, norm2_w, w_ffn_gate, w_ffn_up, w_ffn_down, final_norm_w):
    bp, seq, _ = x_prompt.shape
    bs, dseq, _ = x_sample.shape
    ffn_rows = min(FFN_ROWS, bp * seq)
    assert seq % PROMPT_ROWS == 0 and PROMPT_ROWS % CHUNK == 0 and (bp * seq) % ffn_rows == 0
    assert dseq <= CHUNK and dseq % (2 * SUB) == 0
    split = [0, A_WIDTH, 2 * A_WIDTH, 2 * A_WIDTH + B_WIDTH, 2 * A_WIDTH + B_WIDTH + B_CONV_DIM]
    split.append(split[-1] + B_HEADS)
    for _ in range(4):
        split.append(split[-1] + C_WIDTH)

    xp = x_prompt
    xs = x_sample.reshape(1, bs * dseq, D_MODEL)
    p_out = ([], [], [], [], [])
    s_out = ([], [], [], [], [])
    for l in range(DEPTH):
        w = w_in[l]
        pieces = [w[:, split[i]:split[i + 1]] for i in range(9)]
        xa_w, ga_w, z_w, xbc_w, dt_w, q_w, f_w, i_w, g_w = pieces
        win = jnp.concatenate([xa_w, ga_w, z_w, xbc_w, q_w, f_w, i_w, g_w,
                               jnp.pad(dt_w, ((0, 0), (0, LANES - B_HEADS)))], axis=1).astype(BF16)
        mixer_weights = [
            _row(norm1_w[l]), win,
            rglru_conv_w[l], _row(rglru_conv_b[l]),
            jnp.concatenate([_block_diag(rglru_wa[l]), _block_diag(rglru_wx[l])], axis=1).astype(BF16),
            _row(jnp.concatenate([rglru_ba[l], rglru_bx[l]])), _row(rglru_lambda[l]),
            ssd_conv_w[l], _row(ssd_conv_b[l]), _head_row(ssd_dt_bias[l]), _head_row(ssd_a_log[l]),
            _row(jnp.repeat(ssd_d[l], B_HEAD_DIM)), _row(ssd_norm_w[l]),
            hgrn_lb.astype(F32), _row(hgrn_norm_w[l]), w_out[l].astype(BF16),
        ]
        lane_head = jnp.arange(B_WIDTH) // B_HEAD_DIM
        spread = (jnp.arange(LANES)[:, None] == lane_head[None, :]).astype(BF16)
        chan_head = jnp.arange(C_WIDTH) // C_KDIM
        head_ones = (chan_head[:, None] == chan_head[None, :]).astype(BF16)
        p_consts = [_chunk_tri(PROMPT_ROWS, CHUNK), spread, head_ones]
        s_consts = [_chunk_tri(bs * dseq, dseq), spread, head_ones]
        ffn_weights = [_row(norm2_w[l]), w_ffn_gate[l].astype(BF16), w_ffn_up[l].astype(BF16),
                       w_ffn_down[l].astype(BF16), _row(final_norm_w)]
        final = l == DEPTH - 1

        p_states = [jnp.zeros((bp, HALO, A_WIDTH), F32), jnp.zeros((bp, 1, A_WIDTH), F32),
                    jnp.zeros((bp, HALO, B_CONV_DIM), F32),
                    jnp.zeros((bp, B_GROUPS, B_STATE, B_GROUP_WIDTH), F32),
                    jnp.zeros((bp, C_WIDTH, C_WIDTH), F32)]
        xp, *new_p = _mixer_call(l, xp, p_states, mixer_weights + p_consts, 1, PROMPT_ROWS, CHUNK)
        xp = _ffn_call(final, xp.reshape(bp * seq, D_MODEL), ffn_weights, ffn_rows, f"prompt_l{l}")
        xp = xp.reshape(bp, seq, D_MODEL)

        s_states = [_conv_state_to_kernel(state_rglru_conv[l]), state_rglru_h[l].reshape(bs, 1, A_WIDTH),
                    _conv_state_to_kernel(state_ssd_conv[l]), _ssd_state_to_kernel(state_ssd[l]),
                    _hgrn_state_to_kernel(state_hgrn[l])]
        xs, *new_s = _mixer_call(l, xs, s_states, mixer_weights + s_consts, bs, dseq, dseq)
        xs = _ffn_call(final, xs.reshape(bs * dseq, D_MODEL), ffn_weights, bs * dseq, f"sample_l{l}")
        xs = xs.reshape(1, bs * dseq, D_MODEL)

        for acc, new in ((p_out, new_p), (s_out, new_s)):
            conv_a, h_a, conv_b, s_b, s_c = new
            acc[0].append(conv_a[:, HALO - (CONV_W - 1):])
            acc[1].append(h_a.reshape(-1, A_WIDTH))
            acc[2].append(conv_b[:, HALO - (CONV_W - 1):])
            acc[3].append(_ssd_state_from_kernel(s_b))
            acc[4].append(_hgrn_state_from_kernel(s_c))
    y_prompt = xp
    y_sample = xs.reshape(bs, dseq, D_MODEL)
    return (y_prompt, y_sample, *(jnp.stack(v) for v in p_out), *(jnp.stack(v) for v in s_out))
```

```python
import functools

import jax
import jax.numpy as jnp
from jax import lax
from jax.experimental import pallas as pl
from jax.experimental.pallas import tpu as pltpu

F32 = jnp.float32
BF16 = jnp.bfloat16

D_MODEL = 1024
DEPTH = 2
CHUNK = 64
EPS = 1e-6
CONV_W = 4
A_WIDTH = 256
A_HEADS = 4
A_HEAD_DIM = A_WIDTH // A_HEADS
LRU_C = 8.0
B_WIDTH = 512
B_HEAD_DIM = 64
B_HEADS = B_WIDTH // B_HEAD_DIM
B_GROUPS = 2
B_HPG = B_HEADS // B_GROUPS
B_STATE = 128
B_GROUP_WIDTH = B_WIDTH // B_GROUPS
B_CONV_DIM = B_WIDTH + 2 * B_GROUPS * B_STATE
C_WIDTH = 256
C_HEADS = 4
C_KDIM = C_WIDTH // C_HEADS
D_FF = 2816

SUBLANES = 8
LANES = 128
HALO = SUBLANES
SUB = SUBLANES
PROMPT_ROWS = 256
PROMPT_SEQS = 2
FFN_ROWS = 512
VMEM_LIMIT_BYTES = 56 * 1024 * 1024

OFF_XA = 0
OFF_GA = OFF_XA + A_WIDTH
OFF_Z = OFF_GA + A_WIDTH
OFF_XBC = OFF_Z + B_WIDTH
OFF_Q = OFF_XBC + B_CONV_DIM
OFF_F = OFF_Q + C_WIDTH
OFF_I = OFF_F + C_WIDTH
OFF_G = OFF_I + C_WIDTH
OFF_DT = OFF_G + C_WIDTH
IN_COLS_PADDED = OFF_DT + LANES


def _dot(a, b):
    return jnp.dot(a.astype(BF16), b.astype(BF16), preferred_element_type=F32)


def _dot_nt(a, b):
    return lax.dot_general(a.astype(BF16), b.astype(BF16), (((1,), (1,)), ((), ())),
                           preferred_element_type=F32)


def _dot_tn(a, b):
    return lax.dot_general(a.astype(BF16), b.astype(BF16), (((0,), (0,)), ((), ())),
                           preferred_element_type=F32)


def _rows(shape):
    return lax.broadcasted_iota(jnp.int32, shape, 0)


def _lanes(shape):
    return lax.broadcasted_iota(jnp.int32, shape, 1)


def _rmsnorm(x, w):
    return x * lax.rsqrt(jnp.mean(x * x, axis=-1, keepdims=True) + EPS) * w


def _cat_rows(parts):
    return parts[0] if len(parts) == 1 else jnp.concatenate(parts, axis=0)


def _shift_rows(x, prev, d):
    r = pltpu.roll(x, d, 0)
    p = pltpu.roll(prev, d, 0)
    head = jnp.where(_rows(prev.shape) < d, p, r[:HALO])
    if x.shape[0] == HALO:
        return head
    return jnp.concatenate([head, r[HALO:]], axis=0)


def _causal_conv(x, prev, w, b):
    out = b + x * w[CONV_W - 1:CONV_W]
    for d in range(1, CONV_W):
        out = out + _shift_rows(x, prev, d) * w[CONV_W - 1 - d:CONV_W - d]
    return out


def _tiles(x):
    n, c = x.shape
    return x.reshape(n // SUBLANES, SUBLANES, c)


def _split_bf16(x, terms):
    pieces = []
    rest = x
    for i in range(terms):
        piece = rest.astype(BF16)
        if i + 1 < terms:
            rest = rest - piece.astype(F32)
        pieces.append(piece)
    return pieces


def _cumsum_rows(tri, x):
    n = tri.shape[0]
    segs = []
    for r in range(0, x.shape[0], n):
        out = None
        for piece in _split_bf16(x[r:r + n], 3):
            part = jnp.dot(tri, piece, preferred_element_type=F32)
            out = part if out is None else out + part
        segs.append(out)
    return _cat_rows(segs)


def _linear_scan(a, u, h0):
    a3 = _tiles(a)
    u3 = _tiles(u)
    pos3 = lax.broadcasted_iota(jnp.int32, a3.shape, 1)
    d = 1
    while d < SUBLANES:
        keep = pos3 >= d
        u3 = jnp.where(keep, a3 * pltpu.roll(u3, d, 1) + u3, u3)
        a3 = jnp.where(keep, a3 * pltpu.roll(a3, d, 1), a3)
        d *= 2
    carry = h0
    tiles = []
    for j in range(a3.shape[0]):
        h_j = u3[j] + a3[j] * carry
        carry = h_j[SUBLANES - 1:]
        tiles.append(h_j)
    return jnp.concatenate(tiles, axis=0)


def _pad_rows(x, n):
    if x.shape[0] == n:
        return x
    return jnp.concatenate([x, jnp.zeros((n - x.shape[0], x.shape[1]), x.dtype)], axis=0)


def _tile_rows(x, reps):
    return jnp.concatenate([x] * reps, axis=0)


def _expand_heads(x, spread, terms):
    out = None
    for piece in _split_bf16(x, terms):
        part = jnp.dot(piece, spread, preferred_element_type=F32)
        out = part if out is None else out + part
    return out


def _ssd_segment(xdt, acs, bm, cm, st, consts, cs):
    diag_m, causal_m, blockdiag = consts
    n_chunks = xdt.shape[0] // cs
    prep = []
    for c in range(n_chunks):
        sl = slice(c * cs, (c + 1) * cs)
        a_c = acs[sl]
        acs_s = jnp.sum(jnp.where(diag_m, a_c, 0.0), axis=0, keepdims=True)
        decay = jnp.exp(jnp.where(causal_m, a_c - acs_s, -jnp.inf))
        last = a_c[cs - 1:cs]
        prep.append((sl, a_c, decay, last, xdt[sl] * jnp.exp(last - a_c)))
    cb, upd = [], []
    for sl, a_c, decay, last, xt in prep:
        for g in range(B_GROUPS):
            ns = slice(g * B_STATE, (g + 1) * B_STATE)
            ws = slice(g * B_GROUP_WIDTH, (g + 1) * B_GROUP_WIDTH)
            cb.append(_dot_nt(cm[sl, ns], _tile_rows(_pad_rows(bm[sl, ns], CHUNK), B_HPG)))
            upd.append(_dot_tn(bm[sl, ns], xt[:, ws]))
    states = [list(st)]
    for c, (sl, a_c, decay, last, xt) in enumerate(prep):
        states.append([jnp.exp(last[:, g * B_GROUP_WIDTH:(g + 1) * B_GROUP_WIDTH]) * states[c][g]
                       + upd[c * B_GROUPS + g] for g in range(B_GROUPS)])
    y_rows = []
    for c, (sl, a_c, decay, last, xt) in enumerate(prep):
        y_parts = []
        for g in range(B_GROUPS):
            ns = slice(g * B_STATE, (g + 1) * B_STATE)
            ws = slice(g * B_GROUP_WIDTH, (g + 1) * B_GROUP_WIDTH)
            w = cb[c * B_GROUPS + g] * decay[:, ws]
            xbd = jnp.where(blockdiag, _tile_rows(_pad_rows(xdt[sl, ws], CHUNK), B_HPG), 0.0)
            y_parts.append(_dot(w, xbd) + _dot(cm[sl, ns], states[c][g]) * jnp.exp(a_c[:, ws]))
        y_rows.append(jnp.concatenate(y_parts, axis=1))
    return _cat_rows(y_rows), states[-1]


def _hgrn_segment(q, k, v, bc, st, blockdiag, cs):
    n_chunks = q.shape[0] // cs
    half = cs // 2
    levels = []
    m = SUB
    while m <= half:
        levels.append(m)
        m *= 2
    shape = (C_HEADS * half, C_WIDTH)
    head_m = (_rows(shape) // half) == (_lanes(shape) // C_KDIM)
    jobs, q_in, k_upd, decay_last = [], [], [], []
    for c in range(n_chunks):
        base = c * cs
        bc_c = bc[base:base + cs]
        bl = bc_c[cs - 1:cs]
        q_in.append(q[base:base + cs] * jnp.exp(bc_c))
        k_upd.append(k[base:base + cs] * jnp.exp(bl - bc_c))
        decay_last.append(jnp.exp(bl))
        for m in levels:
            n_blk = cs // (2 * m)
            q_rows, k_rows, v_rows = [], [], []
            for b in range(n_blk):
                lo, mid, hi = base + 2 * m * b, base + 2 * m * b + m, base + 2 * m * (b + 1)
                e = bc[mid - 1:mid]
                k_rows.append(k[lo:mid] * jnp.exp(e - bc[lo:mid]))
                v_rows.append(v[lo:mid])
                q_rows.append(q[mid:hi] * jnp.exp(bc[mid:hi] - e))
            kbd = jnp.where(head_m, _tile_rows(_cat_rows(k_rows), C_HEADS), 0.0)
            vbd = jnp.where(head_m, _tile_rows(_cat_rows(v_rows), C_HEADS), 0.0)
            jobs.append((c, m, n_blk, _cat_rows(q_rows), kbd, vbd))
    att = [_dot_nt(q_m, kbd) for (_, _, _, q_m, kbd, _) in jobs]
    upd = [_dot_tn(v[c * cs:(c + 1) * cs], k_upd[c]) for c in range(n_chunks)]
    states = [st]
    for c in range(n_chunks):
        states.append(states[c] * decay_last[c] + jnp.where(blockdiag, upd[c], 0.0))
    pieces = []
    for c in range(n_chunks):
        o_c = _dot_nt(q_in[c], states[c])
        pieces.extend(o_c[i * SUB:(i + 1) * SUB] for i in range(cs // SUB))
    for (c, m, n_blk, _, _, vbd), a in zip(jobs, att):
        if n_blk > 1:
            same = (_rows(a.shape) // m) == ((_lanes(a.shape) % half) // m)
            a = jnp.where(same, a, 0.0)
        o_m = _dot(a, vbd)
        for b in range(n_blk):
            for i in range(m // SUB):
                dst = c * (cs // SUB) + (2 * m * b + m) // SUB + i
                src = b * m + i * SUB
                pieces[dst] = pieces[dst] + o_m[src:src + SUB]
    return _cat_rows(pieces), states[-1]


def _mixer_kernel(layer, nb, ts, cs,
                  x_ref, ca0_ref, ha0_ref, cb0_ref, sb0_ref, sc0_ref,
                  n1w_ref, win_ref, acw_ref, acb_ref, wgate_ref, bgate_ref, lam_ref,
                  bcw_ref, bcb_ref, dtb_ref, alog_ref, dskip_ref, bnw_ref,
                  lb_ref, cnw_ref, wout_ref, tri_ref, spread_ref, ones_ref,
                  xo_ref, cao_ref, hao_ref, cbo_ref, sbo_ref, sco_ref,
                  ca_s, ha_s, cb_s, sb_s, sc_s):
    t = pl.program_id(1)
    rows = nb * ts

    @pl.when(t == 0)
    def _():
        ca_s[...] = ca0_ref[...]
        ha_s[...] = ha0_ref[...]
        cb_s[...] = cb0_ref[...]
        sb_s[...] = sb0_ref[...]
        sc_s[...] = sc0_ref[...]

    x = x_ref[0].reshape(rows, D_MODEL)
    h = _rmsnorm(x, n1w_ref[...]).astype(BF16)

    def proj(off, width):
        return jnp.dot(h, win_ref[:, off:off + width], preferred_element_type=F32)

    def seg(v, s):
        return v[s * ts:(s + 1) * ts]

    xa = proj(OFF_XA, A_WIDTH)
    xbc = proj(OFF_XBC, B_CONV_DIM)

    acw = acw_ref[...]
    xc = _cat_rows([_causal_conv(seg(xa, s), ca_s[s], acw, acb_ref[...]) for s in range(nb)])
    for s in range(nb):
        ca_s[s] = seg(xa, s)[ts - HALO:]
    gates = _dot(xc, wgate_ref[...]) + bgate_ref[...]
    dt_raw = proj(OFF_DT, LANES)
    fpre = proj(OFF_F, C_WIDTH)
    q_raw = proj(OFF_Q, C_WIDTH)
    v = proj(OFF_I, C_WIDTH)
    ga = proj(OFF_GA, A_WIDTH)
    z = proj(OFF_Z, B_WIDTH)
    og = proj(OFF_G, C_WIDTH)
    r_gate = jax.nn.sigmoid(gates[:, :A_WIDTH])
    i_gate = jax.nn.sigmoid(gates[:, A_WIDTH:])
    log_a = -LRU_C * r_gate * jax.nn.softplus(-lam_ref[...])
    a = jnp.exp(log_a)
    u = jnp.sqrt(-jnp.tanh(log_a) * (a * a + 1.0)) * (i_gate * xc)
    ha_parts = []
    for s in range(nb):
        h_seq = _linear_scan(seg(a, s), seg(u, s), ha_s[s])
        ha_s[s] = h_seq[ts - 1:]
        ha_parts.append(h_seq)
    ya = _cat_rows(ha_parts) * jax.nn.gelu(ga)

    bcw = bcw_ref[...]
    xbc_c = _cat_rows([_causal_conv(seg(xbc, s), cb_s[s], bcw, bcb_ref[...]) for s in range(nb)])
    for s in range(nb):
        cb_s[s] = seg(xbc, s)[ts - HALO:]
    xbc_c = jax.nn.silu(xbc_c)
    xs = xbc_c[:, :B_WIDTH]
    bm = xbc_c[:, B_WIDTH:B_WIDTH + B_GROUPS * B_STATE]
    cm = xbc_c[:, B_WIDTH + B_GROUPS * B_STATE:]
    dt_c = jax.nn.softplus(dt_raw + dtb_ref[...])
    acs_c = _cumsum_rows(tri_ref[...], dt_c * (-jnp.exp(alog_ref[...])))
    xdt = xs * _expand_heads(dt_c, spread_ref[...], 2)
    acs = _expand_heads(acs_c, spread_ref[...], 3)
    pos512 = _lanes((cs, B_WIDTH)) % B_HEAD_DIM
    row512 = _rows((cs, B_WIDTH))
    sq = (B_GROUP_WIDTH, B_GROUP_WIDTH)
    blockdiag = (_rows(sq) // B_HEAD_DIM) == (_lanes(sq) // B_HEAD_DIM)
    ssd_consts = (row512 == pos512, row512 >= pos512, blockdiag)
    yb_parts = []
    for s in range(nb):
        y_s, st = _ssd_segment(seg(xdt, s), seg(acs, s), seg(bm, s), seg(cm, s),
                               [sb_s[s, g] for g in range(B_GROUPS)], ssd_consts, cs)
        yb_parts.append(y_s)
        for g in range(B_GROUPS):
            sb_s[s, g] = st[g]
    yb = _cat_rows(yb_parts) + xs * dskip_ref[...]
    vb = yb * jax.nn.silu(z)
    vb2 = vb * vb
    inv = [jnp.broadcast_to(
        lax.rsqrt(jnp.mean(vb2[:, g * B_GROUP_WIDTH:(g + 1) * B_GROUP_WIDTH], axis=-1, keepdims=True) + EPS),
        (rows, B_GROUP_WIDTH)) for g in range(B_GROUPS)]
    yb = vb * jnp.concatenate(inv, axis=1) * bnw_ref[...]

    lb_all = lb_ref[...]
    lb_e = jnp.exp(lb_all - jnp.max(lb_all, axis=0, keepdims=True))
    lb_sm = lb_e / jnp.sum(lb_e, axis=0, keepdims=True)
    lb_cum = lb_sm[0:1]
    lb_first = lb_cum
    for d in range(1, layer + 1):
        lb_cum = lb_cum + lb_sm[d:d + 1]
    lb = lb_cum - lb_first
    q = jax.nn.silu(q_raw)
    e_f = jnp.exp(-jnp.abs(fpre))
    log_sig = jnp.minimum(fpre, 0.0) - jnp.log(1.0 + e_f)
    sig_neg = jnp.where(fpre > 0.0, e_f, 1.0) / (1.0 + e_f)
    log_f = jnp.logaddexp(jnp.log(lb), jnp.log1p(-lb) + log_sig)
    k = (1.0 - lb) * sig_neg
    bc = _cumsum_rows(tri_ref[...], log_f)
    o_parts = []
    for s in range(nb):
        o_s, sc_s[s] = _hgrn_segment(seg(q, s), seg(k, s), seg(v, s), seg(bc, s), sc_s[s], blockdiag, cs)
        o_parts.append(o_s)
    o3 = _tiles(_cat_rows(o_parts))
    head_ones = ones_ref[...]
    q3, k3, v3, bc3 = _tiles(q), _tiles(k), _tiles(v), _tiles(bc)
    pos3 = lax.broadcasted_iota(jnp.int32, q3.shape, 1)
    for d in range(SUB):
        if d == 0:
            p3 = q3 * k3
            v_d = v3
        else:
            w = jnp.exp(jnp.where(pos3 >= d, bc3 - pltpu.roll(bc3, d, 1), -jnp.inf))
            p3 = q3 * pltpu.roll(k3, d, 1) * w
            v_d = pltpu.roll(v3, d, 1)
        o3 = o3 + _tiles(_dot(p3.reshape(rows, C_WIDTH), head_ones)) * v_d
    o = o3.reshape(rows, C_WIDTH)
    yc = o * lax.rsqrt(jnp.mean(o * o, axis=-1, keepdims=True) + EPS) * cnw_ref[...] * jax.nn.silu(og)

    mix = jnp.concatenate([ya, yb, yc], axis=1)
    xo_ref[0] = (x + _dot(mix, wout_ref[...])).reshape(nb, ts, D_MODEL)

    @pl.when(t == pl.num_programs(1) - 1)
    def _():
        cao_ref[...] = ca_s[...]
        hao_ref[...] = ha_s[...]
        cbo_ref[...] = cb_s[...]
        sbo_ref[...] = sb_s[...]
        sco_ref[...] = sc_s[...]


def _ffn_kernel(final, x_ref, n2w_ref, wg_ref, wu_ref, wd_ref, fw_ref, o_ref):
    x = x_ref[...]
    h = _rmsnorm(x, n2w_ref[...]).astype(BF16)
    gate = jnp.dot(h, wg_ref[...], preferred_element_type=F32)
    up = jnp.dot(h, wu_ref[...], preferred_element_type=F32)
    y = x + _dot(jax.nn.silu(gate) * up, wd_ref[...])
    if final:
        y = _rmsnorm(y, fw_ref[...])
    o_ref[...] = y


def _const_spec(shape):
    zeros = (0,) * len(shape)
    return pl.BlockSpec(shape, lambda *_: zeros, pipeline_mode=pl.Buffered(1))


def _mixer_call(layer, x, states, weights, nb, ts, cs):
    groups, _, seq_len, _ = x.shape
    n_steps = seq_len // ts
    x_spec = pl.BlockSpec((1, nb, ts, D_MODEL), lambda g, t: (g, 0, t, 0))

    def state_spec(a):
        shape = (nb,) + a.shape[1:]
        tail = (0,) * (a.ndim - 1)
        return pl.BlockSpec(shape, lambda g, t: (g,) + tail)

    state_specs = [state_spec(a) for a in states]
    state_shapes = [jax.ShapeDtypeStruct(a.shape, F32) for a in states]
    scratch = [pltpu.VMEM((nb,) + a.shape[1:], F32) for a in states]
    return pl.pallas_call(
        functools.partial(_mixer_kernel, layer, nb, ts, cs),
        grid=(groups, n_steps),
        in_specs=[x_spec] + state_specs + [_const_spec(w.shape) for w in weights],
        out_specs=[x_spec] + state_specs,
        out_shape=[jax.ShapeDtypeStruct(x.shape, F32)] + state_shapes,
        scratch_shapes=scratch,
        compiler_params=pltpu.CompilerParams(
            dimension_semantics=("arbitrary", "arbitrary"), vmem_limit_bytes=VMEM_LIMIT_BYTES),
        name=f"mixer_l{layer}_nb{nb}",
    )(x, *states, *weights)


def _ffn_call(final, x, weights, block_rows, tag):
    n_rows = x.shape[0]
    x_spec = pl.BlockSpec((block_rows, D_MODEL), lambda i: (i, 0))
    return pl.pallas_call(
        functools.partial(_ffn_kernel, final),
        grid=(n_rows // block_rows,),
        in_specs=[x_spec] + [_const_spec(w.shape) for w in weights],
        out_specs=x_spec,
        out_shape=jax.ShapeDtypeStruct(x.shape, F32),
        compiler_params=pltpu.CompilerParams(
            dimension_semantics=("arbitrary",), vmem_limit_bytes=VMEM_LIMIT_BYTES),
        name=f"ffn_{tag}",
    )(x, *weights)


def _block_diag(w):
    heads, n_i, n_j = w.shape
    eye = jnp.eye(heads, dtype=w.dtype)
    return jnp.einsum("hij,hg->higj", w, eye).reshape(heads * n_i, heads * n_j)


def _ssd_state_to_kernel(s):
    b = s.shape[0]
    s = s.reshape(b, B_GROUPS, B_HPG, B_HEAD_DIM, B_STATE)
    return s.transpose(0, 1, 4, 2, 3).reshape(b, B_GROUPS, B_STATE, B_GROUP_WIDTH)


def _ssd_state_from_kernel(s):
    b = s.shape[0]
    s = s.reshape(b, B_GROUPS, B_STATE, B_HPG, B_HEAD_DIM)
    return s.transpose(0, 1, 3, 4, 2).reshape(b, B_HEADS, B_HEAD_DIM, B_STATE)


def _hgrn_state_to_kernel(s):
    b = s.shape[0]
    eye = jnp.eye(C_HEADS, dtype=s.dtype)
    return jnp.einsum("bhkv,hg->bhvgk", s, eye).reshape(b, C_WIDTH, C_WIDTH)


def _hgrn_state_from_kernel(s):
    b = s.shape[0]
    return jnp.einsum("bhvhk->bhkv", s.reshape(b, C_HEADS, C_KDIM, C_HEADS, C_KDIM))


def _conv_state_to_kernel(s):
    return jnp.pad(s, ((0, 0), (HALO - (CONV_W - 1), 0), (0, 0)))


def _row(v):
    return v.reshape(1, -1).astype(F32)


def _chunk_tri(rows, chunk):
    t = jnp.arange(rows)[:, None]
    s = jnp.arange(rows)[None, :]
    return ((s <= t) & (s // chunk == t // chunk)).astype(BF16)


def _head_row(v):
    return jnp.pad(_row(v), ((0, 0), (0, LANES - B_HEADS)))


def kernel(x_prompt, x_sample, state_rglru_conv, state_rglru_h, state_ssd_conv, state_ssd, state_hgrn,
           norm1_w, w_in, rglru_conv_w, rglru_conv_b, rglru_wa, rglru_ba, rglru_wx, rglru_bx, rglru_lambda,
           ssd_conv_w, ssd_conv_b, ssd_dt_bias, ssd_a_log, ssd_d, ssd_norm_w, hgrn_lb, hgrn_norm_w,
           w_out, norm2_w, w_ffn_gate, w_ffn_up, w_ffn_down, final_norm_w):
    bp, seq, _ = x_prompt.shape
    bs, dseq, _ = x_sample.shape
    ffn_rows = min(FFN_ROWS, bp * seq)
    assert seq % PROMPT_ROWS == 0 and PROMPT_ROWS % CHUNK == 0 and (bp * seq) % ffn_rows == 0
    assert dseq <= CHUNK and dseq % (2 * SUB) == 0
    split = [0, A_WIDTH, 2 * A_WIDTH, 2 * A_WIDTH + B_WIDTH, 2 * A_WIDTH + B_WIDTH + B_CONV_DIM]
    split.append(split[-1] + B_HEADS)
    for _ in range(4):
        split.append(split[-1] + C_WIDTH)

    xp = x_prompt
    xs = x_sample.reshape(1, bs, dseq, D_MODEL)
    nbp = PROMPT_SEQS if bp % PROMPT_SEQS == 0 else 1
    p_out = ([], [], [], [], [])
    s_out = ([], [], [], [], [])
    for l in range(DEPTH):
        w = w_in[l]
        pieces = [w[:, split[i]:split[i + 1]] for i in range(9)]
        xa_w, ga_w, z_w, xbc_w, dt_w, q_w, f_w, i_w, g_w = pieces
        win = jnp.concatenate([xa_w, ga_w, z_w, xbc_w, q_w, f_w, i_w, g_w,
                               jnp.pad(dt_w, ((0, 0), (0, LANES - B_HEADS)))], axis=1).astype(BF16)
        mixer_weights = [
            _row(norm1_w[l]), win,
            rglru_conv_w[l], _row(rglru_conv_b[l]),
            jnp.concatenate([_block_diag(rglru_wa[l]), _block_diag(rglru_wx[l])], axis=1).astype(BF16),
            _row(jnp.concatenate([rglru_ba[l], rglru_bx[l]])), _row(rglru_lambda[l]),
            ssd_conv_w[l], _row(ssd_conv_b[l]), _head_row(ssd_dt_bias[l]), _head_row(ssd_a_log[l]),
            _row(jnp.repeat(ssd_d[l], B_HEAD_DIM)), _row(ssd_norm_w[l]),
            hgrn_lb.astype(F32), _row(hgrn_norm_w[l]), w_out[l].astype(BF16),
        ]
        lane_head = jnp.arange(B_WIDTH) // B_HEAD_DIM
        spread = (jnp.arange(LANES)[:, None] == lane_head[None, :]).astype(BF16)
        chan_head = jnp.arange(C_WIDTH) // C_KDIM
        head_ones = (chan_head[:, None] == chan_head[None, :]).astype(BF16)
        p_consts = [_chunk_tri(PROMPT_ROWS, CHUNK), spread, head_ones]
        s_consts = [_chunk_tri(bs * dseq, dseq), spread, head_ones]
        ffn_weights = [_row(norm2_w[l]), w_ffn_gate[l].astype(BF16), w_ffn_up[l].astype(BF16),
                       w_ffn_down[l].astype(BF16), _row(final_norm_w)]
        final = l == DEPTH - 1

        p_states = [jnp.zeros((bp, HALO, A_WIDTH), F32), jnp.zeros((bp, 1, A_WIDTH), F32),
                    jnp.zeros((bp, HALO, B_CONV_DIM), F32),
                    jnp.zeros((bp, B_GROUPS, B_STATE, B_GROUP_WIDTH), F32),
                    jnp.zeros((bp, C_WIDTH, C_WIDTH), F32)]
        xp, *new_p = _mixer_call(l, xp.reshape(bp // nbp, nbp, seq, D_MODEL), p_states,
                                 mixer_weights + p_consts, nbp, PROMPT_ROWS, CHUNK)
        xp = _ffn_call(final, xp.reshape(bp * seq, D_MODEL), ffn_weights, ffn_rows, f"prompt_l{l}")
        xp = xp.reshape(bp, seq, D_MODEL)

        s_states = [_conv_state_to_kernel(state_rglru_conv[l]), state_rglru_h[l].reshape(bs, 1, A_WIDTH),
                    _conv_state_to_kernel(state_ssd_conv[l]), _ssd_state_to_kernel(state_ssd[l]),
                    _hgrn_state_to_kernel(state_hgrn[l])]
        xs, *new_s = _mixer_call(l, xs, s_states, mixer_weights + s_consts, bs, dseq, dseq)
        xs = _ffn_call(final, xs.reshape(bs * dseq, D_MODEL), ffn_weights, bs * dseq, f"sample_l{l}")
        xs = xs.reshape(1, bs, dseq, D_MODEL)

        for acc, new in ((p_out, new_p), (s_out, new_s)):
            conv_a, h_a, conv_b, s_b, s_c = new
            acc[0].append(conv_a[:, HALO - (CONV_W - 1):])
            acc[1].append(h_a.reshape(-1, A_WIDTH))
            acc[2].append(conv_b[:, HALO - (CONV_W - 1):])
            acc[3].append(_ssd_state_from_kernel(s_b))
            acc[4].append(_hgrn_state_from_kernel(s_c))
    y_prompt = xp
    y_sample = xs.reshape(bs, dseq, D_MODEL)
    return (y_prompt, y_sample, *(jnp.stack(v) for v in p_out), *(jnp.stack(v) for v in s_out))
```

```python
import functools

import jax
import jax.numpy as jnp
from jax import lax
from jax.experimental import pallas as pl
from jax.experimental.pallas import tpu as pltpu

F32 = jnp.float32
BF16 = jnp.bfloat16

D_MODEL = 1024
DEPTH = 2
CHUNK = 64
EPS = 1e-6
LOG2E = 1.4426950408889634
CONV_W = 4
A_WIDTH = 256
A_HEADS = 4
A_HEAD_DIM = A_WIDTH // A_HEADS
LRU_C = 8.0
B_WIDTH = 512
B_HEAD_DIM = 64
B_HEADS = B_WIDTH // B_HEAD_DIM
B_GROUPS = 2
B_HPG = B_HEADS // B_GROUPS
B_STATE = 128
B_GROUP_WIDTH = B_WIDTH // B_GROUPS
B_CONV_DIM = B_WIDTH + 2 * B_GROUPS * B_STATE
C_WIDTH = 256
C_HEADS = 4
C_KDIM = C_WIDTH // C_HEADS
D_FF = 2816

SUBLANES = 8
LANES = 128
HALO = SUBLANES
SUB = SUBLANES
PROMPT_ROWS = 256
PROMPT_SEQS = 2
FFN_ROWS = 512
HEAD_COPIES = 5
VMEM_LIMIT_BYTES = 56 * 1024 * 1024

OFF_XA = 0
OFF_GA = OFF_XA + A_WIDTH
OFF_Z = OFF_GA + A_WIDTH
OFF_XBC = OFF_Z + B_WIDTH
OFF_Q = OFF_XBC + B_CONV_DIM
OFF_F = OFF_Q + C_WIDTH
OFF_I = OFF_F + C_WIDTH
OFF_G = OFF_I + C_WIDTH
OFF_DT = OFF_G + C_WIDTH
IN_COLS_PADDED = OFF_DT + LANES


def _dot(a, b):
    return jnp.dot(a.astype(BF16), b.astype(BF16), preferred_element_type=F32)


def _dot_nt(a, b):
    return lax.dot_general(a.astype(BF16), b.astype(BF16), (((1,), (1,)), ((), ())),
                           preferred_element_type=F32)


def _dot_tn(a, b):
    return lax.dot_general(a.astype(BF16), b.astype(BF16), (((0,), (0,)), ((), ())),
                           preferred_element_type=F32)


def _rows(shape):
    return lax.broadcasted_iota(jnp.int32, shape, 0)


def _lanes(shape):
    return lax.broadcasted_iota(jnp.int32, shape, 1)


def _rmsnorm(x, w):
    return x * lax.rsqrt(jnp.mean(x * x, axis=-1, keepdims=True) + EPS) * w


def _cat_rows(parts):
    return parts[0] if len(parts) == 1 else jnp.concatenate(parts, axis=0)


def _shift_rows(x, prev, d):
    r = pltpu.roll(x, d, 0)
    p = pltpu.roll(prev, d, 0)
    head = jnp.where(_rows(prev.shape) < d, p, r[:HALO])
    if x.shape[0] == HALO:
        return head
    return jnp.concatenate([head, r[HALO:]], axis=0)


def _causal_conv(x, prev, w, b):
    out = b + x * w[CONV_W - 1:CONV_W]
    for d in range(1, CONV_W):
        out = out + _shift_rows(x, prev, d) * w[CONV_W - 1 - d:CONV_W - d]
    return out


def _tiles(x):
    n, c = x.shape
    return x.reshape(n // SUBLANES, SUBLANES, c)


def _split_bf16(x, terms):
    pieces = []
    rest = x
    for i in range(terms):
        piece = rest.astype(BF16)
        if i + 1 < terms:
            rest = rest - piece.astype(F32)
        pieces.append(piece)
    return pieces


def _cumsum_rows(tri, x):
    n = tri.shape[0]
    segs = []
    for r in range(0, x.shape[0], n):
        out = None
        for piece in _split_bf16(x[r:r + n], 3):
            part = jnp.dot(tri, piece, preferred_element_type=F32)
            out = part if out is None else out + part
        segs.append(out)
    return _cat_rows(segs)


def _linear_scan(a, u, h0):
    a3 = _tiles(a)
    u3 = _tiles(u)
    pos3 = lax.broadcasted_iota(jnp.int32, a3.shape, 1)
    d = 1
    while d < SUBLANES:
        keep = pos3 >= d
        u3 = jnp.where(keep, a3 * pltpu.roll(u3, d, 1) + u3, u3)
        a3 = jnp.where(keep, a3 * pltpu.roll(a3, d, 1), a3)
        d *= 2
    carry = h0
    tiles = []
    for j in range(a3.shape[0]):
        h_j = u3[j] + a3[j] * carry
        carry = h_j[SUBLANES - 1:]
        tiles.append(h_j)
    return jnp.concatenate(tiles, axis=0)


def _pad_rows(x, n):
    if x.shape[0] == n:
        return x
    return jnp.concatenate([x, jnp.zeros((n - x.shape[0], x.shape[1]), x.dtype)], axis=0)


def _tile_rows(x, reps):
    return jnp.concatenate([x] * reps, axis=0)


def _expand_heads(dt_c, acs_c, spread):
    dt_hi, dt_lo = (p.astype(F32) for p in _split_bf16(dt_c, 2))
    acs_hi, acs_lo, acs_lo2 = (p.astype(F32) for p in _split_bf16(acs_c, 3))
    copy = _lanes(dt_c.shape) // B_HEADS
    packed = jnp.where(copy == 0, dt_hi, jnp.where(copy == 1, dt_lo, jnp.where(
        copy == 2, acs_hi, jnp.where(copy == 3, acs_lo, acs_lo2))))
    both = jnp.dot(packed.astype(BF16), spread, preferred_element_type=F32)
    return both[:, :B_WIDTH], both[:, B_WIDTH:]


def _ssd_segment(xdt, acs, bm, cm, st, consts, cs):
    diag_m, causal_m, blockdiag = consts
    n_chunks = xdt.shape[0] // cs
    prep = []
    for c in range(n_chunks):
        sl = slice(c * cs, (c + 1) * cs)
        a_c = acs[sl]
        acs_s = jnp.sum(jnp.where(diag_m, a_c, 0.0), axis=0, keepdims=True)
        decay = jnp.exp2(jnp.where(causal_m, a_c - acs_s, -jnp.inf))
        last = a_c[cs - 1:cs]
        prep.append((sl, a_c, decay, last, xdt[sl] * jnp.exp2(last - a_c)))
    cb, upd = [], []
    for sl, a_c, decay, last, xt in prep:
        for g in range(B_GROUPS):
            ns = slice(g * B_STATE, (g + 1) * B_STATE)
            ws = slice(g * B_GROUP_WIDTH, (g + 1) * B_GROUP_WIDTH)
            cb.append(_dot_nt(cm[sl, ns], _tile_rows(_pad_rows(bm[sl, ns], CHUNK), B_HPG)))
            upd.append(_dot_tn(bm[sl, ns], xt[:, ws]))
    states = [list(st)]
    for c, (sl, a_c, decay, last, xt) in enumerate(prep):
        states.append([jnp.exp2(last[:, g * B_GROUP_WIDTH:(g + 1) * B_GROUP_WIDTH]) * states[c][g]
                       + upd[c * B_GROUPS + g] for g in range(B_GROUPS)])
    y_rows = []
    for c, (sl, a_c, decay, last, xt) in enumerate(prep):
        y_parts = []
        for g in range(B_GROUPS):
            ns = slice(g * B_STATE, (g + 1) * B_STATE)
            ws = slice(g * B_GROUP_WIDTH, (g + 1) * B_GROUP_WIDTH)
            w = cb[c * B_GROUPS + g] * decay[:, ws]
            xbd = jnp.where(blockdiag, _tile_rows(_pad_rows(xdt[sl, ws], CHUNK), B_HPG), 0.0)
            y_parts.append(_dot(w, xbd) + _dot(cm[sl, ns], states[c][g]) * jnp.exp2(a_c[:, ws]))
        y_rows.append(jnp.concatenate(y_parts, axis=1))
    return _cat_rows(y_rows), states[-1]


def _hgrn_segment(q, k, v, bc, st, blockdiag, cs):
    n_chunks = q.shape[0] // cs
    half = cs // 2
    levels = []
    m = SUB
    while m <= half:
        levels.append(m)
        m *= 2
    shape = (C_HEADS * half, C_WIDTH)
    head_m = (_rows(shape) // half) == (_lanes(shape) // C_KDIM)
    jobs, q_in, k_upd, decay_last = [], [], [], []
    for c in range(n_chunks):
        base = c * cs
        bc_c = bc[base:base + cs]
        bl = bc_c[cs - 1:cs]
        q_in.append(q[base:base + cs] * jnp.exp2(bc_c))
        k_upd.append(k[base:base + cs] * jnp.exp2(bl - bc_c))
        decay_last.append(jnp.exp2(bl))
        for m in levels:
            n_blk = cs // (2 * m)
            q_rows, k_rows, v_rows = [], [], []
            for b in range(n_blk):
                lo, mid, hi = base + 2 * m * b, base + 2 * m * b + m, base + 2 * m * (b + 1)
                e = bc[mid - 1:mid]
                k_rows.append(k[lo:mid] * jnp.exp2(e - bc[lo:mid]))
                v_rows.append(v[lo:mid])
                q_rows.append(q[mid:hi] * jnp.exp2(bc[mid:hi] - e))
            kbd = jnp.where(head_m, _tile_rows(_cat_rows(k_rows), C_HEADS), 0.0)
            vbd = jnp.where(head_m, _tile_rows(_cat_rows(v_rows), C_HEADS), 0.0)
            jobs.append((c, m, n_blk, _cat_rows(q_rows), kbd, vbd))
    att = [_dot_nt(q_m, kbd) for (_, _, _, q_m, kbd, _) in jobs]
    upd = [_dot_tn(v[c * cs:(c + 1) * cs], k_upd[c]) for c in range(n_chunks)]
    states = [st]
    for c in range(n_chunks):
        states.append(states[c] * decay_last[c] + jnp.where(blockdiag, upd[c], 0.0))
    pieces = []
    for c in range(n_chunks):
        o_c = _dot_nt(q_in[c], states[c])
        pieces.extend(o_c[i * SUB:(i + 1) * SUB] for i in range(cs // SUB))
    for (c, m, n_blk, _, _, vbd), a in zip(jobs, att):
        if n_blk > 1:
            same = (_rows(a.shape) // m) == ((_lanes(a.shape) % half) // m)
            a = jnp.where(same, a, 0.0)
        o_m = _dot(a, vbd)
        for b in range(n_blk):
            for i in range(m // SUB):
                dst = c * (cs // SUB) + (2 * m * b + m) // SUB + i
                src = b * m + i * SUB
                pieces[dst] = pieces[dst] + o_m[src:src + SUB]
    return _cat_rows(pieces), states[-1]


def _mixer_kernel(layer, nb, ts, cs,
                  x_ref, ca0_ref, ha0_ref, cb0_ref, sb0_ref, sc0_ref,
                  n1w_ref, win_ref, acw_ref, acb_ref, wgate_ref, bgate_ref, lam_ref,
                  bcw_ref, bcb_ref, dtb_ref, alog_ref, dskip_ref, bnw_ref,
                  lb_ref, cnw_ref, wout_ref, tri_ref, spread_ref, ones_ref,
                  xo_ref, cao_ref, hao_ref, cbo_ref, sbo_ref, sco_ref,
                  ca_s, ha_s, cb_s, sb_s, sc_s):
    t = pl.program_id(1)
    rows = nb * ts

    @pl.when(t == 0)
    def _():
        ca_s[...] = ca0_ref[...]
        ha_s[...] = ha0_ref[...]
        cb_s[...] = cb0_ref[...]
        sb_s[...] = sb0_ref[...]
        sc_s[...] = sc0_ref[...]

    x = x_ref[0].reshape(rows, D_MODEL)
    h = _rmsnorm(x, n1w_ref[...]).astype(BF16)

    def proj(off, width):
        return jnp.dot(h, win_ref[:, off:off + width], preferred_element_type=F32)

    def seg(v, s):
        return v[s * ts:(s + 1) * ts]

    xa = proj(OFF_XA, A_WIDTH)
    xbc = proj(OFF_XBC, B_CONV_DIM)

    acw = acw_ref[...]
    xc = _cat_rows([_causal_conv(seg(xa, s), ca_s[s], acw, acb_ref[...]) for s in range(nb)])
    for s in range(nb):
        ca_s[s] = seg(xa, s)[ts - HALO:]
    gates = _dot(xc, wgate_ref[...]) + bgate_ref[...]
    dt_raw = proj(OFF_DT, LANES)
    fpre = proj(OFF_F, C_WIDTH)
    q_raw = proj(OFF_Q, C_WIDTH)
    v = proj(OFF_I, C_WIDTH)
    ga = proj(OFF_GA, A_WIDTH)
    z = proj(OFF_Z, B_WIDTH)
    og = proj(OFF_G, C_WIDTH)
    r_gate = jax.nn.sigmoid(gates[:, :A_WIDTH])
    i_gate = jax.nn.sigmoid(gates[:, A_WIDTH:])
    log_a = -LRU_C * r_gate * jax.nn.softplus(-lam_ref[...])
    a = jnp.exp(log_a)
    one_minus_a2 = -jnp.tanh(log_a) * (a * a + 1.0)
    u = jnp.where(one_minus_a2 > 0.0, one_minus_a2 * lax.rsqrt(one_minus_a2), 0.0) * (i_gate * xc)
    ha_parts = []
    for s in range(nb):
        h_seq = _linear_scan(seg(a, s), seg(u, s), ha_s[s])
        ha_s[s] = h_seq[ts - 1:]
        ha_parts.append(h_seq)
    ya = _cat_rows(ha_parts) * jax.nn.gelu(ga)

    bcw = bcw_ref[...]
    xbc_c = _cat_rows([_causal_conv(seg(xbc, s), cb_s[s], bcw, bcb_ref[...]) for s in range(nb)])
    for s in range(nb):
        cb_s[s] = seg(xbc, s)[ts - HALO:]
    xbc_c = jax.nn.silu(xbc_c)
    xs = xbc_c[:, :B_WIDTH]
    bm = xbc_c[:, B_WIDTH:B_WIDTH + B_GROUPS * B_STATE]
    cm = xbc_c[:, B_WIDTH + B_GROUPS * B_STATE:]
    dt_c = jax.nn.softplus(dt_raw + dtb_ref[...])
    acs_c = _cumsum_rows(tri_ref[...], dt_c * (-LOG2E * jnp.exp(alog_ref[...])))
    dt, acs = _expand_heads(dt_c, acs_c, spread_ref[...])
    xdt = xs * dt
    pos512 = _lanes((cs, B_WIDTH)) % B_HEAD_DIM
    row512 = _rows((cs, B_WIDTH))
    sq = (B_GROUP_WIDTH, B_GROUP_WIDTH)
    blockdiag = (_rows(sq) // B_HEAD_DIM) == (_lanes(sq) // B_HEAD_DIM)
    ssd_consts = (row512 == pos512, row512 >= pos512, blockdiag)
    yb_parts = []
    for s in range(nb):
        y_s, st = _ssd_segment(seg(xdt, s), seg(acs, s), seg(bm, s), seg(cm, s),
                               [sb_s[s, g] for g in range(B_GROUPS)], ssd_consts, cs)
        yb_parts.append(y_s)
        for g in range(B_GROUPS):
            sb_s[s, g] = st[g]
    yb = _cat_rows(yb_parts) + xs * dskip_ref[...]
    vb = yb * jax.nn.silu(z)
    vb2 = vb * vb
    inv = [jnp.broadcast_to(
        lax.rsqrt(jnp.mean(vb2[:, g * B_GROUP_WIDTH:(g + 1) * B_GROUP_WIDTH], axis=-1, keepdims=True) + EPS),
        (rows, B_GROUP_WIDTH)) for g in range(B_GROUPS)]
    yb = vb * jnp.concatenate(inv, axis=1) * bnw_ref[...]

    lb_all = lb_ref[...]
    lb_e = jnp.exp(lb_all - jnp.max(lb_all, axis=0, keepdims=True))
    lb_sm = lb_e / jnp.sum(lb_e, axis=0, keepdims=True)
    lb_cum = lb_sm[0:1]
    lb_first = lb_cum
    for d in range(1, layer + 1):
        lb_cum = lb_cum + lb_sm[d:d + 1]
    lb = lb_cum - lb_first
    q = jax.nn.silu(q_raw)
    e_f = jnp.exp(-jnp.abs(fpre))
    log_sig = jnp.minimum(fpre, 0.0) - jnp.log(1.0 + e_f)
    sig_neg = jnp.where(fpre > 0.0, e_f, 1.0) / (1.0 + e_f)
    y_lb = jnp.log(lb) - fpre
    log_f = log_sig + jnp.maximum(y_lb, 0.0) + jnp.log(1.0 + jnp.exp(-jnp.abs(y_lb)))
    k = (1.0 - lb) * sig_neg
    bc = _cumsum_rows(tri_ref[...], log_f * LOG2E)
    o_parts = []
    for s in range(nb):
        o_s, sc_s[s] = _hgrn_segment(seg(q, s), seg(k, s), seg(v, s), seg(bc, s), sc_s[s], blockdiag, cs)
        o_parts.append(o_s)
    o3 = _tiles(_cat_rows(o_parts))
    head_ones = ones_ref[...]
    q3, k3, v3, bc3 = _tiles(q), _tiles(k), _tiles(v), _tiles(bc)
    pos3 = lax.broadcasted_iota(jnp.int32, q3.shape, 1)
    for d in range(SUB):
        if d == 0:
            p3 = q3 * k3
            v_d = v3
        else:
            w = jnp.exp2(jnp.where(pos3 >= d, bc3 - pltpu.roll(bc3, d, 1), -jnp.inf))
            p3 = q3 * pltpu.roll(k3, d, 1) * w
            v_d = pltpu.roll(v3, d, 1)
        o3 = o3 + _tiles(_dot(p3.reshape(rows, C_WIDTH), head_ones)) * v_d
    o = o3.reshape(rows, C_WIDTH)
    yc = o * lax.rsqrt(jnp.mean(o * o, axis=-1, keepdims=True) + EPS) * cnw_ref[...] * jax.nn.silu(og)

    mix = jnp.concatenate([ya, yb, yc], axis=1)
    xo_ref[0] = (x + _dot(mix, wout_ref[...])).reshape(nb, ts, D_MODEL)

    @pl.when(t == pl.num_programs(1) - 1)
    def _():
        cao_ref[...] = ca_s[...]
        hao_ref[...] = ha_s[...]
        cbo_ref[...] = cb_s[...]
        sbo_ref[...] = sb_s[...]
        sco_ref[...] = sc_s[...]


def _ffn_kernel(final, x_ref, n2w_ref, wg_ref, wu_ref, wd_ref, fw_ref, o_ref):
    x = x_ref[...]
    h = _rmsnorm(x, n2w_ref[...]).astype(BF16)
    gate = jnp.dot(h, wg_ref[...], preferred_element_type=F32)
    up = jnp.dot(h, wu_ref[...], preferred_element_type=F32)
    y = x + _dot(jax.nn.silu(gate) * up, wd_ref[...])
    if final:
        y = _rmsnorm(y, fw_ref[...])
    o_ref[...] = y


def _const_spec(shape):
    zeros = (0,) * len(shape)
    return pl.BlockSpec(shape, lambda *_: zeros, pipeline_mode=pl.Buffered(1))


def _mixer_call(layer, x, states, weights, nb, ts, cs):
    groups, _, seq_len, _ = x.shape
    n_steps = seq_len // ts
    x_spec = pl.BlockSpec((1, nb, ts, D_MODEL), lambda g, t: (g, 0, t, 0))

    def state_spec(a):
        shape = (nb,) + a.shape[1:]
        tail = (0,) * (a.ndim - 1)
        return pl.BlockSpec(shape, lambda g, t: (g,) + tail)

    state_specs = [state_spec(a) for a in states]
    state_shapes = [jax.ShapeDtypeStruct(a.shape, F32) for a in states]
    scratch = [pltpu.VMEM((nb,) + a.shape[1:], F32) for a in states]
    return pl.pallas_call(
        functools.partial(_mixer_kernel, layer, nb, ts, cs),
        grid=(groups, n_steps),
        in_specs=[x_spec] + state_specs + [_const_spec(w.shape) for w in weights],
        out_specs=[x_spec] + state_specs,
        out_shape=[jax.ShapeDtypeStruct(x.shape, F32)] + state_shapes,
        scratch_shapes=scratch,
        compiler_params=pltpu.CompilerParams(
            dimension_semantics=("arbitrary", "arbitrary"), vmem_limit_bytes=VMEM_LIMIT_BYTES),
        name=f"mixer_l{layer}_nb{nb}",
    )(x, *states, *weights)


def _ffn_call(final, x, weights, block_rows, tag):
    n_rows = x.shape[0]
    x_spec = pl.BlockSpec((block_rows, D_MODEL), lambda i: (i, 0))
    return pl.pallas_call(
        functools.partial(_ffn_kernel, final),
        grid=(n_rows // block_rows,),
        in_specs=[x_spec] + [_const_spec(w.shape) for w in weights],
        out_specs=x_spec,
        out_shape=jax.ShapeDtypeStruct(x.shape, F32),
        compiler_params=pltpu.CompilerParams(
            dimension_semantics=("arbitrary",), vmem_limit_bytes=VMEM_LIMIT_BYTES),
        name=f"ffn_{tag}",
    )(x, *weights)


def _block_diag(w):
    heads, n_i, n_j = w.shape
    eye = jnp.eye(heads, dtype=w.dtype)
    return jnp.einsum("hij,hg->higj", w, eye).reshape(heads * n_i, heads * n_j)


def _ssd_state_to_kernel(s):
    b = s.shape[0]
    s = s.reshape(b, B_GROUPS, B_HPG, B_HEAD_DIM, B_STATE)
    return s.transpose(0, 1, 4, 2, 3).reshape(b, B_GROUPS, B_STATE, B_GROUP_WIDTH)


def _ssd_state_from_kernel(s):
    b = s.shape[0]
    s = s.reshape(b, B_GROUPS, B_STATE, B_HPG, B_HEAD_DIM)
    return s.transpose(0, 1, 3, 4, 2).reshape(b, B_HEADS, B_HEAD_DIM, B_STATE)


def _hgrn_state_to_kernel(s):
    b = s.shape[0]
    eye = jnp.eye(C_HEADS, dtype=s.dtype)
    return jnp.einsum("bhkv,hg->bhvgk", s, eye).reshape(b, C_WIDTH, C_WIDTH)


def _hgrn_state_from_kernel(s):
    b = s.shape[0]
    return jnp.einsum("bhvhk->bhkv", s.reshape(b, C_HEADS, C_KDIM, C_HEADS, C_KDIM))


def _conv_state_to_kernel(s):
    return jnp.pad(s, ((0, 0), (HALO - (CONV_W - 1), 0), (0, 0)))


def _row(v):
    return v.reshape(1, -1).astype(F32)


def _chunk_tri(rows, chunk):
    t = jnp.arange(rows)[:, None]
    s = jnp.arange(rows)[None, :]
    return ((s <= t) & (s // chunk == t // chunk)).astype(BF16)


def _head_spread():
    r = jnp.arange(LANES)[:, None]
    col = jnp.arange(2 * B_WIDTH)[None, :]
    same_head = (r % B_HEADS) == ((col % B_WIDTH) // B_HEAD_DIM)
    half = jnp.where(r // B_HEADS < 2, 0, 1)
    live = r < HEAD_COPIES * B_HEADS
    return (same_head & (half == col // B_WIDTH) & live).astype(BF16)


def _head_row(v):
    return jnp.pad(_row(jnp.tile(v, HEAD_COPIES)), ((0, 0), (0, LANES - HEAD_COPIES * B_HEADS)))


def kernel(x_prompt, x_sample, state_rglru_conv, state_rglru_h, state_ssd_conv, state_ssd, state_hgrn,
           norm1_w, w_in, rglru_conv_w, rglru_conv_b, rglru_wa, rglru_ba, rglru_wx, rglru_bx, rglru_lambda,
           ssd_conv_w, ssd_conv_b, ssd_dt_bias, ssd_a_log, ssd_d, ssd_norm_w, hgrn_lb, hgrn_norm_w,
           w_out, norm2_w, w_ffn_gate, w_ffn_up, w_ffn_down, final_norm_w):
    bp, seq, _ = x_prompt.shape
    bs, dseq, _ = x_sample.shape
    ffn_rows = min(FFN_ROWS, bp * seq)
    assert seq % PROMPT_ROWS == 0 and PROMPT_ROWS % CHUNK == 0 and (bp * seq) % ffn_rows == 0
    assert dseq <= CHUNK and dseq % (2 * SUB) == 0
    split = [0, A_WIDTH, 2 * A_WIDTH, 2 * A_WIDTH + B_WIDTH, 2 * A_WIDTH + B_WIDTH + B_CONV_DIM]
    split.append(split[-1] + B_HEADS)
    for _ in range(4):
        split.append(split[-1] + C_WIDTH)

    xp = x_prompt
    xs = x_sample.reshape(1, bs, dseq, D_MODEL)
    nbp = PROMPT_SEQS if bp % PROMPT_SEQS == 0 else 1
    p_out = ([], [], [], [], [])
    s_out = ([], [], [], [], [])
    for l in range(DEPTH):
        w = w_in[l]
        pieces = [w[:, split[i]:split[i + 1]] for i in range(9)]
        xa_w, ga_w, z_w, xbc_w, dt_w, q_w, f_w, i_w, g_w = pieces
        win = jnp.concatenate([xa_w, ga_w, z_w, xbc_w, q_w, f_w, i_w, g_w,
                               jnp.pad(jnp.tile(dt_w, (1, HEAD_COPIES)), ((0, 0), (0, LANES - HEAD_COPIES * B_HEADS)))],
                              axis=1).astype(BF16)
        mixer_weights = [
            _row(norm1_w[l]), win,
            rglru_conv_w[l], _row(rglru_conv_b[l]),
            jnp.concatenate([_block_diag(rglru_wa[l]), _block_diag(rglru_wx[l])], axis=1).astype(BF16),
            _row(jnp.concatenate([rglru_ba[l], rglru_bx[l]])), _row(rglru_lambda[l]),
            ssd_conv_w[l], _row(ssd_conv_b[l]), _head_row(ssd_dt_bias[l]), _head_row(ssd_a_log[l]),
            _row(jnp.repeat(ssd_d[l], B_HEAD_DIM)), _row(ssd_norm_w[l]),
            hgrn_lb.astype(F32), _row(hgrn_norm_w[l]), w_out[l].astype(BF16),
        ]
        spread = _head_spread()
        chan_head = jnp.arange(C_WIDTH) // C_KDIM
        head_ones = (chan_head[:, None] == chan_head[None, :]).astype(BF16)
        p_consts = [_chunk_tri(PROMPT_ROWS, CHUNK), spread, head_ones]
        s_consts = [_chunk_tri(bs * dseq, dseq), spread, head_ones]
        ffn_weights = [_row(norm2_w[l]), w_ffn_gate[l].astype(BF16), w_ffn_up[l].astype(BF16),
                       w_ffn_down[l].astype(BF16), _row(final_norm_w)]
        final = l == DEPTH - 1

        p_states = [jnp.zeros((bp, HALO, A_WIDTH), F32), jnp.zeros((bp, 1, A_WIDTH), F32),
                    jnp.zeros((bp, HALO, B_CONV_DIM), F32),
                    jnp.zeros((bp, B_GROUPS, B_STATE, B_GROUP_WIDTH), F32),
                    jnp.zeros((bp, C_WIDTH, C_WIDTH), F32)]
        xp, *new_p = _mixer_call(l, xp.reshape(bp // nbp, nbp, seq, D_MODEL), p_states,
                                 mixer_weights + p_consts, nbp, PROMPT_ROWS, CHUNK)
        xp = _ffn_call(final, xp.reshape(bp * seq, D_MODEL), ffn_weights, ffn_rows, f"prompt_l{l}")
        xp = xp.reshape(bp, seq, D_MODEL)

        s_states = [_conv_state_to_kernel(state_rglru_conv[l]), state_rglru_h[l].reshape(bs, 1, A_WIDTH),
                    _conv_state_to_kernel(state_ssd_conv[l]), _ssd_state_to_kernel(state_ssd[l]),
                    _hgrn_state_to_kernel(state_hgrn[l])]
        xs, *new_s = _mixer_call(l, xs, s_states, mixer_weights + s_consts, bs, dseq, dseq)
        xs = _ffn_call(final, xs.reshape(bs * dseq, D_MODEL), ffn_weights, bs * dseq, f"sample_l{l}")
        xs = xs.reshape(1, bs, dseq, D_MODEL)

        for acc, new in ((p_out, new_p), (s_out, new_s)):
            conv_a, h_a, conv_b, s_b, s_c = new
            acc[0].append(conv_a[:, HALO - (CONV_W - 1):])
            acc[1].append(h_a.reshape(-1, A_WIDTH))
            acc[2].append(conv_b[:, HALO - (CONV_W - 1):])
            acc[3].append(_ssd_state_from_kernel(s_b))
            acc[4].append(_hgrn_state_from_kernel(s_c))
    y_prompt = xp
    y_sample = xs.reshape(bs, dseq, D_MODEL)
    return (y_prompt, y_sample, *(jnp.stack(v) for v in p_out), *(jnp.stack(v) for v in s_out))
```

```python
import functools

import jax
import jax.numpy as jnp
from jax import lax
from jax.experimental import pallas as pl
from jax.experimental.pallas import tpu as pltpu

F32 = jnp.float32
BF16 = jnp.bfloat16

D_MODEL = 1024
DEPTH = 2
CHUNK = 64
EPS = 1e-6
LOG2E = 1.4426950408889634
CONV_W = 4
A_WIDTH = 256
A_HEADS = 4
A_HEAD_DIM = A_WIDTH // A_HEADS
LRU_C = 8.0
B_WIDTH = 512
B_HEAD_DIM = 64
B_HEADS = B_WIDTH // B_HEAD_DIM
B_GROUPS = 2
B_HPG = B_HEADS // B_GROUPS
B_STATE = 128
B_GROUP_WIDTH = B_WIDTH // B_GROUPS
B_CONV_DIM = B_WIDTH + 2 * B_GROUPS * B_STATE
C_WIDTH = 256
C_HEADS = 4
C_KDIM = C_WIDTH // C_HEADS
D_FF = 2816
N_STATES = 5

SUBLANES = 8
LANES = 128
HALO = SUBLANES
SUB = SUBLANES
PROMPT_ROWS = 256
PROMPT_SEQS = 2
FFN_ROWS = 512
HEAD_COPIES = 5
VMEM_LIMIT_BYTES = 56 * 1024 * 1024

OFF_XA = 0
OFF_GA = OFF_XA + A_WIDTH
OFF_Z = OFF_GA + A_WIDTH
OFF_XBC = OFF_Z + B_WIDTH
OFF_Q = OFF_XBC + B_CONV_DIM
OFF_F = OFF_Q + C_WIDTH
OFF_I = OFF_F + C_WIDTH
OFF_G = OFF_I + C_WIDTH
OFF_DT = OFF_G + C_WIDTH
IN_COLS_PADDED = OFF_DT + LANES


def _dot(a, b):
    return jnp.dot(a.astype(BF16), b.astype(BF16), preferred_element_type=F32)


def _dot_nt(a, b):
    return lax.dot_general(a.astype(BF16), b.astype(BF16), (((1,), (1,)), ((), ())),
                           preferred_element_type=F32)


def _dot_tn(a, b):
    return lax.dot_general(a.astype(BF16), b.astype(BF16), (((0,), (0,)), ((), ())),
                           preferred_element_type=F32)


def _rows(shape):
    return lax.broadcasted_iota(jnp.int32, shape, 0)


def _lanes(shape):
    return lax.broadcasted_iota(jnp.int32, shape, 1)


def _rmsnorm(x, w):
    return x * lax.rsqrt(jnp.mean(x * x, axis=-1, keepdims=True) + EPS) * w


def _cat_rows(parts):
    return parts[0] if len(parts) == 1 else jnp.concatenate(parts, axis=0)


def _shift_rows(x, prev, d):
    r = pltpu.roll(x, d, 0)
    p = pltpu.roll(prev, d, 0)
    head = jnp.where(_rows(prev.shape) < d, p, r[:HALO])
    if x.shape[0] == HALO:
        return head
    return jnp.concatenate([head, r[HALO:]], axis=0)


def _causal_conv(x, prev, w, b):
    out = b + x * w[CONV_W - 1:CONV_W]
    for d in range(1, CONV_W):
        out = out + _shift_rows(x, prev, d) * w[CONV_W - 1 - d:CONV_W - d]
    return out


def _tiles(x):
    n, c = x.shape
    return x.reshape(n // SUBLANES, SUBLANES, c)


def _split_bf16(x, terms):
    pieces = []
    rest = x
    for i in range(terms):
        piece = rest.astype(BF16)
        if i + 1 < terms:
            rest = rest - piece.astype(F32)
        pieces.append(piece)
    return pieces


def _cumsum_rows(tri, x):
    n = tri.shape[0]
    segs = []
    for r in range(0, x.shape[0], n):
        out = None
        for piece in _split_bf16(x[r:r + n], 3):
            part = jnp.dot(tri, piece, preferred_element_type=F32)
            out = part if out is None else out + part
        segs.append(out)
    return _cat_rows(segs)


def _linear_scan(a, u, h0):
    a3 = _tiles(a)
    u3 = _tiles(u)
    pos3 = lax.broadcasted_iota(jnp.int32, a3.shape, 1)
    d = 1
    while d < SUBLANES:
        keep = pos3 >= d
        u3 = jnp.where(keep, a3 * pltpu.roll(u3, d, 1) + u3, u3)
        a3 = jnp.where(keep, a3 * pltpu.roll(a3, d, 1), a3)
        d *= 2
    carry = h0
    tiles = []
    for j in range(a3.shape[0]):
        h_j = u3[j] + a3[j] * carry
        carry = h_j[SUBLANES - 1:]
        tiles.append(h_j)
    return jnp.concatenate(tiles, axis=0)


def _pad_rows(x, n):
    if x.shape[0] == n:
        return x
    return jnp.concatenate([x, jnp.zeros((n - x.shape[0], x.shape[1]), x.dtype)], axis=0)


def _tile_rows(x, reps):
    return jnp.concatenate([x] * reps, axis=0)


def _expand_heads(dt_c, acs_c, spread):
    dt_hi, dt_lo = (p.astype(F32) for p in _split_bf16(dt_c, 2))
    acs_hi, acs_lo, acs_lo2 = (p.astype(F32) for p in _split_bf16(acs_c, 3))
    copy = _lanes(dt_c.shape) // B_HEADS
    packed = jnp.where(copy == 0, dt_hi, jnp.where(copy == 1, dt_lo, jnp.where(
        copy == 2, acs_hi, jnp.where(copy == 3, acs_lo, acs_lo2))))
    both = jnp.dot(packed.astype(BF16), spread, preferred_element_type=F32)
    return both[:, :B_WIDTH], both[:, B_WIDTH:]


def _ssd_segment(xdt, acs, bm, cm, st, consts, cs):
    diag_m, causal_m, blockdiag = consts
    n_chunks = xdt.shape[0] // cs
    prep = []
    for c in range(n_chunks):
        sl = slice(c * cs, (c + 1) * cs)
        a_c = acs[sl]
        acs_s = jnp.sum(jnp.where(diag_m, a_c, 0.0), axis=0, keepdims=True)
        decay = jnp.exp2(jnp.where(causal_m, a_c - acs_s, -jnp.inf))
        last = a_c[cs - 1:cs]
        prep.append((sl, a_c, decay, last, xdt[sl] * jnp.exp2(last - a_c)))
    cb, upd = [], []
    for sl, a_c, decay, last, xt in prep:
        for g in range(B_GROUPS):
            ns = slice(g * B_STATE, (g + 1) * B_STATE)
            ws = slice(g * B_GROUP_WIDTH, (g + 1) * B_GROUP_WIDTH)
            cb.append(_dot_nt(cm[sl, ns], _tile_rows(_pad_rows(bm[sl, ns], CHUNK), B_HPG)))
            upd.append(_dot_tn(bm[sl, ns], xt[:, ws]))
    states = [list(st)]
    for c, (sl, a_c, decay, last, xt) in enumerate(prep):
        states.append([jnp.exp2(last[:, g * B_GROUP_WIDTH:(g + 1) * B_GROUP_WIDTH]) * states[c][g]
                       + upd[c * B_GROUPS + g] for g in range(B_GROUPS)])
    y_rows = []
    for c, (sl, a_c, decay, last, xt) in enumerate(prep):
        y_parts = []
        for g in range(B_GROUPS):
            ns = slice(g * B_STATE, (g + 1) * B_STATE)
            ws = slice(g * B_GROUP_WIDTH, (g + 1) * B_GROUP_WIDTH)
            w = cb[c * B_GROUPS + g] * decay[:, ws]
            xbd = jnp.where(blockdiag, _tile_rows(_pad_rows(xdt[sl, ws], CHUNK), B_HPG), 0.0)
            y_parts.append(_dot(w, xbd) + _dot(cm[sl, ns], states[c][g]) * jnp.exp2(a_c[:, ws]))
        y_rows.append(jnp.concatenate(y_parts, axis=1))
    return _cat_rows(y_rows), states[-1]


def _hgrn_segment(q, k, v, bc, st, blockdiag, cs):
    n_chunks = q.shape[0] // cs
    half = cs // 2
    levels = []
    m = SUB
    while m <= half:
        levels.append(m)
        m *= 2
    shape = (C_HEADS * half, C_WIDTH)
    head_m = (_rows(shape) // half) == (_lanes(shape) // C_KDIM)
    jobs, q_in, k_upd, decay_last = [], [], [], []
    for c in range(n_chunks):
        base = c * cs
        bc_c = bc[base:base + cs]
        bl = bc_c[cs - 1:cs]
        q_in.append(q[base:base + cs] * jnp.exp2(bc_c))
        k_upd.append(k[base:base + cs] * jnp.exp2(bl - bc_c))
        decay_last.append(jnp.exp2(bl))
        for m in levels:
            n_blk = cs // (2 * m)
            q_rows, k_rows, v_rows = [], [], []
            for b in range(n_blk):
                lo, mid, hi = base + 2 * m * b, base + 2 * m * b + m, base + 2 * m * (b + 1)
                e = bc[mid - 1:mid]
                k_rows.append(k[lo:mid] * jnp.exp2(e - bc[lo:mid]))
                v_rows.append(v[lo:mid])
                q_rows.append(q[mid:hi] * jnp.exp2(bc[mid:hi] - e))
            kbd = jnp.where(head_m, _tile_rows(_cat_rows(k_rows), C_HEADS), 0.0)
            vbd = jnp.where(head_m, _tile_rows(_cat_rows(v_rows), C_HEADS), 0.0)
            jobs.append((c, m, n_blk, _cat_rows(q_rows), kbd, vbd))
    att = [_dot_nt(q_m, kbd) for (_, _, _, q_m, kbd, _) in jobs]
    upd = [_dot_tn(v[c * cs:(c + 1) * cs], k_upd[c]) for c in range(n_chunks)]
    states = [st]
    for c in range(n_chunks):
        states.append(states[c] * decay_last[c] + jnp.where(blockdiag, upd[c], 0.0))
    pieces = []
    for c in range(n_chunks):
        o_c = _dot_nt(q_in[c], states[c])
        pieces.extend(o_c[i * SUB:(i + 1) * SUB] for i in range(cs // SUB))
    for (c, m, n_blk, _, _, vbd), a in zip(jobs, att):
        if n_blk > 1:
            same = (_rows(a.shape) // m) == ((_lanes(a.shape) % half) // m)
            a = jnp.where(same, a, 0.0)
        o_m = _dot(a, vbd)
        for b in range(n_blk):
            for i in range(m // SUB):
                dst = c * (cs // SUB) + (2 * m * b + m) // SUB + i
                src = b * m + i * SUB
                pieces[dst] = pieces[dst] + o_m[src:src + SUB]
    return _cat_rows(pieces), states[-1]


def _mixer_kernel(layer, nb, ts, cs, has_state, x_ref, *refs):
    refs = list(refs)
    state_in = [refs.pop(0) for _ in range(N_STATES)] if has_state else None
    (n1w_ref, win_ref, acw_ref, acb_ref, wgate_ref, bgate_ref, lam_ref,
     bcw_ref, bcb_ref, dtb_ref, alog_ref, dskip_ref, bnw_ref,
     lb_ref, cnw_ref, wout_ref, tri_ref, spread_ref, ones_ref,
     xo_ref, cao_ref, hao_ref, cbo_ref, sbo_ref, sco_ref,
     ca_s, ha_s, cb_s, sb_s, sc_s) = refs
    t = pl.program_id(1)
    rows = nb * ts
    sq = (B_GROUP_WIDTH, B_GROUP_WIDTH)
    blockdiag = (_rows(sq) // B_HEAD_DIM) == (_lanes(sq) // B_HEAD_DIM)
    hist = slice(HALO - (CONV_W - 1), HALO)

    @pl.when(t == 0)
    def _():
        for ref in (ca_s, ha_s, cb_s, sb_s, sc_s):
            ref[...] = jnp.zeros_like(ref)
        if has_state:
            ca0_ref, ha0_ref, cb0_ref, sb0_ref, sc0_ref = state_in
            ha_s[...] = ha0_ref[...]
            for s in range(nb):
                ca_s[s, hist, :] = ca0_ref[s]
                cb_s[s, hist, :] = cb0_ref[s]
                for g in range(B_GROUPS):
                    sb_s[s, g] = sb0_ref[s, g * B_HPG:(g + 1) * B_HPG].reshape(B_GROUP_WIDTH, B_STATE).T
                kv = sc0_ref[s].reshape(C_WIDTH, C_WIDTH // C_HEADS).T
                sc_s[s] = jnp.where(blockdiag, _tile_rows(kv, C_HEADS), 0.0)

    x = x_ref[0].reshape(rows, D_MODEL)
    h = _rmsnorm(x, n1w_ref[...]).astype(BF16)

    def proj(off, width):
        return jnp.dot(h, win_ref[:, off:off + width], preferred_element_type=F32)

    def seg(v, s):
        return v[s * ts:(s + 1) * ts]

    xa = proj(OFF_XA, A_WIDTH)
    xbc = proj(OFF_XBC, B_CONV_DIM)

    acw = acw_ref[...]
    xc = _cat_rows([_causal_conv(seg(xa, s), ca_s[s], acw, acb_ref[...]) for s in range(nb)])
    for s in range(nb):
        ca_s[s] = seg(xa, s)[ts - HALO:]
    gates = _dot(xc, wgate_ref[...]) + bgate_ref[...]
    dt_raw = proj(OFF_DT, LANES)
    fpre = proj(OFF_F, C_WIDTH)
    q_raw = proj(OFF_Q, C_WIDTH)
    v = proj(OFF_I, C_WIDTH)
    ga = proj(OFF_GA, A_WIDTH)
    z = proj(OFF_Z, B_WIDTH)
    og = proj(OFF_G, C_WIDTH)
    r_gate = jax.nn.sigmoid(gates[:, :A_WIDTH])
    i_gate = jax.nn.sigmoid(gates[:, A_WIDTH:])
    log_a = -LRU_C * r_gate * jax.nn.softplus(-lam_ref[...])
    a = jnp.exp(log_a)
    one_minus_a2 = -jnp.tanh(log_a) * (a * a + 1.0)
    u = jnp.where(one_minus_a2 > 0.0, one_minus_a2 * lax.rsqrt(one_minus_a2), 0.0) * (i_gate * xc)
    ha_parts = []
    for s in range(nb):
        h_seq = _linear_scan(seg(a, s), seg(u, s), ha_s[s])
        ha_s[s] = h_seq[ts - 1:]
        ha_parts.append(h_seq)
    ya = _cat_rows(ha_parts) * jax.nn.gelu(ga)

    bcw = bcw_ref[...]
    xbc_c = _cat_rows([_causal_conv(seg(xbc, s), cb_s[s], bcw, bcb_ref[...]) for s in range(nb)])
    for s in range(nb):
        cb_s[s] = seg(xbc, s)[ts - HALO:]
    xbc_c = jax.nn.silu(xbc_c)
    xs = xbc_c[:, :B_WIDTH]
    bm = xbc_c[:, B_WIDTH:B_WIDTH + B_GROUPS * B_STATE]
    cm = xbc_c[:, B_WIDTH + B_GROUPS * B_STATE:]
    dt_c = jax.nn.softplus(dt_raw + dtb_ref[...])
    acs_c = _cumsum_rows(tri_ref[...], dt_c * (-LOG2E * jnp.exp(alog_ref[...])))
    dt, acs = _expand_heads(dt_c, acs_c, spread_ref[...])
    xdt = xs * dt
    pos512 = _lanes((cs, B_WIDTH)) % B_HEAD_DIM
    row512 = _rows((cs, B_WIDTH))
    ssd_consts = (row512 == pos512, row512 >= pos512, blockdiag)
    yb_parts = []
    for s in range(nb):
        y_s, st = _ssd_segment(seg(xdt, s), seg(acs, s), seg(bm, s), seg(cm, s),
                               [sb_s[s, g] for g in range(B_GROUPS)], ssd_consts, cs)
        yb_parts.append(y_s)
        for g in range(B_GROUPS):
            sb_s[s, g] = st[g]
    yb = _cat_rows(yb_parts) + xs * dskip_ref[...]
    vb = yb * jax.nn.silu(z)
    vb2 = vb * vb
    inv = [jnp.broadcast_to(
        lax.rsqrt(jnp.mean(vb2[:, g * B_GROUP_WIDTH:(g + 1) * B_GROUP_WIDTH], axis=-1, keepdims=True) + EPS),
        (rows, B_GROUP_WIDTH)) for g in range(B_GROUPS)]
    yb = vb * jnp.concatenate(inv, axis=1) * bnw_ref[...]

    lb_all = lb_ref[...]
    lb_e = jnp.exp(lb_all - jnp.max(lb_all, axis=0, keepdims=True))
    lb_sm = lb_e / jnp.sum(lb_e, axis=0, keepdims=True)
    lb_cum = lb_sm[0:1]
    lb_first = lb_cum
    for d in range(1, layer + 1):
        lb_cum = lb_cum + lb_sm[d:d + 1]
    lb = lb_cum - lb_first
    q = jax.nn.silu(q_raw)
    e_f = jnp.exp(-jnp.abs(fpre))
    log_sig = jnp.minimum(fpre, 0.0) - jnp.log(1.0 + e_f)
    sig_neg = jnp.where(fpre > 0.0, e_f, 1.0) / (1.0 + e_f)
    y_lb = jnp.log(lb) - fpre
    log_f = log_sig + jnp.maximum(y_lb, 0.0) + jnp.log(1.0 + jnp.exp(-jnp.abs(y_lb)))
    k = (1.0 - lb) * sig_neg
    bc = _cumsum_rows(tri_ref[...], log_f * LOG2E)
    o_parts = []
    for s in range(nb):
        o_s, sc_s[s] = _hgrn_segment(seg(q, s), seg(k, s), seg(v, s), seg(bc, s), sc_s[s], blockdiag, cs)
        o_parts.append(o_s)
    o3 = _tiles(_cat_rows(o_parts))
    head_ones = ones_ref[...]
    q3, k3, v3, bc3 = _tiles(q), _tiles(k), _tiles(v), _tiles(bc)
    pos3 = lax.broadcasted_iota(jnp.int32, q3.shape, 1)
    for d in range(SUB):
        if d == 0:
            p3 = q3 * k3
            v_d = v3
        else:
            w = jnp.exp2(jnp.where(pos3 >= d, bc3 - pltpu.roll(bc3, d, 1), -jnp.inf))
            p3 = q3 * pltpu.roll(k3, d, 1) * w
            v_d = pltpu.roll(v3, d, 1)
        o3 = o3 + _tiles(_dot(p3.reshape(rows, C_WIDTH), head_ones)) * v_d
    o = o3.reshape(rows, C_WIDTH)
    yc = o * lax.rsqrt(jnp.mean(o * o, axis=-1, keepdims=True) + EPS) * cnw_ref[...] * jax.nn.silu(og)

    mix = jnp.concatenate([ya, yb, yc], axis=1)
    xo_ref[0] = (x + _dot(mix, wout_ref[...])).reshape(nb, ts, D_MODEL)

    @pl.when(t == pl.num_programs(1) - 1)
    def _():
        hao_ref[...] = ha_s[...]
        for s in range(nb):
            cao_ref[s] = ca_s[s, hist, :]
            cbo_ref[s] = cb_s[s, hist, :]
            for g in range(B_GROUPS):
                sbo_ref[s, g * B_HPG:(g + 1) * B_HPG] = sb_s[s, g].T.reshape(B_HPG, B_HEAD_DIM, B_STATE)
            kv = sc_s[s].T
            kv = kv[:, :LANES] + kv[:, LANES:]
            kv = kv + pltpu.roll(kv, C_KDIM, 1)
            sco_ref[s] = kv[:, :C_KDIM].reshape(C_HEADS, C_KDIM, C_KDIM)


def _ffn_kernel(final, x_ref, n2w_ref, wg_ref, wu_ref, wd_ref, fw_ref, o_ref):
    x = x_ref[...]
    h = _rmsnorm(x, n2w_ref[...]).astype(BF16)
    gate = jnp.dot(h, wg_ref[...], preferred_element_type=F32)
    up = jnp.dot(h, wu_ref[...], preferred_element_type=F32)
    y = x + _dot(jax.nn.silu(gate) * up, wd_ref[...])
    if final:
        y = _rmsnorm(y, fw_ref[...])
    o_ref[...] = y


def _const_spec(shape):
    zeros = (0,) * len(shape)
    return pl.BlockSpec(shape, lambda *_: zeros, pipeline_mode=pl.Buffered(1))


def _mixer_call(layer, x, states, weights, nb, ts, cs):
    groups, _, seq_len, _ = x.shape
    n_seqs = groups * nb
    n_steps = seq_len // ts
    x_spec = pl.BlockSpec((1, nb, ts, D_MODEL), lambda g, t: (g, 0, t, 0))
    state_dims = [(CONV_W - 1, A_WIDTH), (1, A_WIDTH), (CONV_W - 1, B_CONV_DIM),
                  (B_HEADS, B_HEAD_DIM, B_STATE), (C_HEADS, C_KDIM, C_KDIM)]
    scratch_dims = [(HALO, A_WIDTH), (1, A_WIDTH), (HALO, B_CONV_DIM),
                    (B_GROUPS, B_STATE, B_GROUP_WIDTH), (C_WIDTH, C_WIDTH)]

    def state_spec(dims):
        tail = (0,) * len(dims)
        return pl.BlockSpec((nb,) + dims, lambda g, t: (g,) + tail)

    state_specs = [state_spec(d) for d in state_dims]
    state_args = [] if states is None else list(states)
    for a, d in zip(state_args, state_dims):
        assert a.shape == (n_seqs,) + d, (a.shape, d)
    return pl.pallas_call(
        functools.partial(_mixer_kernel, layer, nb, ts, cs, states is not None),
        grid=(groups, n_steps),
        in_specs=[x_spec] + state_specs[:len(state_args)] + [_const_spec(w.shape) for w in weights],
        out_specs=[x_spec] + state_specs,
        out_shape=[jax.ShapeDtypeStruct(x.shape, F32)]
        + [jax.ShapeDtypeStruct((n_seqs,) + d, F32) for d in state_dims],
        scratch_shapes=[pltpu.VMEM((nb,) + d, F32) for d in scratch_dims],
        compiler_params=pltpu.CompilerParams(
            dimension_semantics=("arbitrary", "arbitrary"), vmem_limit_bytes=VMEM_LIMIT_BYTES),
        name=f"mixer_l{layer}_nb{nb}",
    )(x, *state_args, *weights)


def _ffn_call(final, x, weights, block_rows, tag):
    n_rows = x.shape[0]
    x_spec = pl.BlockSpec((block_rows, D_MODEL), lambda i: (i, 0))
    return pl.pallas_call(
        functools.partial(_ffn_kernel, final),
        grid=(n_rows // block_rows,),
        in_specs=[x_spec] + [_const_spec(w.shape) for w in weights],
        out_specs=x_spec,
        out_shape=jax.ShapeDtypeStruct(x.shape, F32),
        compiler_params=pltpu.CompilerParams(
            dimension_semantics=("arbitrary",), vmem_limit_bytes=VMEM_LIMIT_BYTES),
        name=f"ffn_{tag}",
    )(x, *weights)


def _block_diag(w):
    heads, n_i, n_j = w.shape
    eye = jnp.eye(heads, dtype=w.dtype)
    return jnp.einsum("hij,hg->higj", w, eye).reshape(heads * n_i, heads * n_j)


def _row(v):
    return v.reshape(1, -1).astype(F32)


def _chunk_tri(rows, chunk):
    t = jnp.arange(rows)[:, None]
    s = jnp.arange(rows)[None, :]
    return ((s <= t) & (s // chunk == t // chunk)).astype(BF16)


def _head_spread():
    r = jnp.arange(LANES)[:, None]
    col = jnp.arange(2 * B_WIDTH)[None, :]
    same_head = (r % B_HEADS) == ((col % B_WIDTH) // B_HEAD_DIM)
    half = jnp.where(r // B_HEADS < 2, 0, 1)
    live = r < HEAD_COPIES * B_HEADS
    return (same_head & (half == col // B_WIDTH) & live).astype(BF16)


def _head_row(v):
    return jnp.pad(_row(jnp.tile(v, HEAD_COPIES)), ((0, 0), (0, LANES - HEAD_COPIES * B_HEADS)))


def kernel(x_prompt, x_sample, state_rglru_conv, state_rglru_h, state_ssd_conv, state_ssd, state_hgrn,
           norm1_w, w_in, rglru_conv_w, rglru_conv_b, rglru_wa, rglru_ba, rglru_wx, rglru_bx, rglru_lambda,
           ssd_conv_w, ssd_conv_b, ssd_dt_bias, ssd_a_log, ssd_d, ssd_norm_w, hgrn_lb, hgrn_norm_w,
           w_out, norm2_w, w_ffn_gate, w_ffn_up, w_ffn_down, final_norm_w):
    bp, seq, _ = x_prompt.shape
    bs, dseq, _ = x_sample.shape
    ffn_rows = min(FFN_ROWS, bp * seq)
    assert seq % PROMPT_ROWS == 0 and PROMPT_ROWS % CHUNK == 0 and (bp * seq) % ffn_rows == 0
    assert dseq <= CHUNK and dseq % (2 * SUB) == 0
    split = [0, A_WIDTH, 2 * A_WIDTH, 2 * A_WIDTH + B_WIDTH, 2 * A_WIDTH + B_WIDTH + B_CONV_DIM]
    split.append(split[-1] + B_HEADS)
    for _ in range(4):
        split.append(split[-1] + C_WIDTH)

    xp = x_prompt
    xs = x_sample.reshape(1, bs, dseq, D_MODEL)
    nbp = PROMPT_SEQS if bp % PROMPT_SEQS == 0 else 1
    p_out = ([], [], [], [], [])
    s_out = ([], [], [], [], [])
    for l in range(DEPTH):
        w = w_in[l]
        pieces = [w[:, split[i]:split[i + 1]] for i in range(9)]
        xa_w, ga_w, z_w, xbc_w, dt_w, q_w, f_w, i_w, g_w = pieces
        win = jnp.concatenate([xa_w, ga_w, z_w, xbc_w, q_w, f_w, i_w, g_w,
                               jnp.pad(jnp.tile(dt_w, (1, HEAD_COPIES)), ((0, 0), (0, LANES - HEAD_COPIES * B_HEADS)))],
                              axis=1).astype(BF16)
        mixer_weights = [
            _row(norm1_w[l]), win,
            rglru_conv_w[l], _row(rglru_conv_b[l]),
            jnp.concatenate([_block_diag(rglru_wa[l]), _block_diag(rglru_wx[l])], axis=1).astype(BF16),
            _row(jnp.concatenate([rglru_ba[l], rglru_bx[l]])), _row(rglru_lambda[l]),
            ssd_conv_w[l], _row(ssd_conv_b[l]), _head_row(ssd_dt_bias[l]), _head_row(ssd_a_log[l]),
            _row(jnp.repeat(ssd_d[l], B_HEAD_DIM)), _row(ssd_norm_w[l]),
            hgrn_lb.astype(F32), _row(hgrn_norm_w[l]), w_out[l].astype(BF16),
        ]
        spread = _head_spread()
        chan_head = jnp.arange(C_WIDTH) // C_KDIM
        head_ones = (chan_head[:, None] == chan_head[None, :]).astype(BF16)
        p_consts = [_chunk_tri(PROMPT_ROWS, CHUNK), spread, head_ones]
        s_consts = [_chunk_tri(bs * dseq, dseq), spread, head_ones]
        ffn_weights = [_row(norm2_w[l]), w_ffn_gate[l].astype(BF16), w_ffn_up[l].astype(BF16),
                       w_ffn_down[l].astype(BF16), _row(final_norm_w)]
        final = l == DEPTH - 1

        xp, *new_p = _mixer_call(l, xp.reshape(bp // nbp, nbp, seq, D_MODEL), None,
                                 mixer_weights + p_consts, nbp, PROMPT_ROWS, CHUNK)
        xp = _ffn_call(final, xp.reshape(bp * seq, D_MODEL), ffn_weights, ffn_rows, f"prompt_l{l}")
        xp = xp.reshape(bp, seq, D_MODEL)

        s_states = [state_rglru_conv[l], state_rglru_h[l].reshape(bs, 1, A_WIDTH), state_ssd_conv[l],
                    state_ssd[l], state_hgrn[l]]
        xs, *new_s = _mixer_call(l, xs, s_states, mixer_weights + s_consts, bs, dseq, dseq)
        xs = _ffn_call(final, xs.reshape(bs * dseq, D_MODEL), ffn_weights, bs * dseq, f"sample_l{l}")
        xs = xs.reshape(1, bs, dseq, D_MODEL)

        for acc, new in ((p_out, new_p), (s_out, new_s)):
            conv_a, h_a, conv_b, s_b, s_c = new
            for dst, val in zip(acc, (conv_a, h_a.reshape(-1, A_WIDTH), conv_b, s_b, s_c)):
                dst.append(val)
    y_prompt = xp
    y_sample = xs.reshape(bs, dseq, D_MODEL)
    return (y_prompt, y_sample, *(jnp.stack(v) for v in p_out), *(jnp.stack(v) for v in s_out))
```

```python
import functools

import jax
import jax.numpy as jnp
from jax import lax
from jax.experimental import pallas as pl
from jax.experimental.pallas import tpu as pltpu

F32 = jnp.float32
BF16 = jnp.bfloat16

D_MODEL = 1024
DEPTH = 2
CHUNK = 64
EPS = 1e-6
LOG2E = 1.4426950408889634
CONV_W = 4
A_WIDTH = 256
A_HEADS = 4
A_HEAD_DIM = A_WIDTH // A_HEADS
LRU_C = 8.0
B_WIDTH = 512
B_HEAD_DIM = 64
B_HEADS = B_WIDTH // B_HEAD_DIM
B_GROUPS = 2
B_HPG = B_HEADS // B_GROUPS
B_STATE = 128
B_GROUP_WIDTH = B_WIDTH // B_GROUPS
B_CONV_DIM = B_WIDTH + 2 * B_GROUPS * B_STATE
C_WIDTH = 256
C_HEADS = 4
C_KDIM = C_WIDTH // C_HEADS
D_FF = 2816
N_STATES = 5

SUBLANES = 8
LANES = 128
HALO = SUBLANES
SUB = SUBLANES
PROMPT_ROWS = 256
PROMPT_SEQS = 2
FFN_ROWS = 1024
FFN_COLS = 768
HEAD_COPIES = 5
VMEM_LIMIT_BYTES = 56 * 1024 * 1024

OFF_XA = 0
OFF_GA = OFF_XA + A_WIDTH
OFF_Z = OFF_GA + A_WIDTH
OFF_XBC = OFF_Z + B_WIDTH
OFF_Q = OFF_XBC + B_CONV_DIM
OFF_F = OFF_Q + C_WIDTH
OFF_I = OFF_F + C_WIDTH
OFF_G = OFF_I + C_WIDTH
OFF_DT = OFF_G + C_WIDTH
IN_COLS_PADDED = OFF_DT + LANES


def _dot(a, b):
    return jnp.dot(a.astype(BF16), b.astype(BF16), preferred_element_type=F32)


def _dot_nt(a, b):
    return lax.dot_general(a.astype(BF16), b.astype(BF16), (((1,), (1,)), ((), ())),
                           preferred_element_type=F32)


def _dot_tn(a, b):
    return lax.dot_general(a.astype(BF16), b.astype(BF16), (((0,), (0,)), ((), ())),
                           preferred_element_type=F32)


def _rows(shape):
    return lax.broadcasted_iota(jnp.int32, shape, 0)


def _lanes(shape):
    return lax.broadcasted_iota(jnp.int32, shape, 1)


def _rmsnorm(x, w):
    return x * lax.rsqrt(jnp.mean(x * x, axis=-1, keepdims=True) + EPS) * w


def _cat_rows(parts):
    return parts[0] if len(parts) == 1 else jnp.concatenate(parts, axis=0)


def _shift_rows(x, prev, d):
    r = pltpu.roll(x, d, 0)
    p = pltpu.roll(prev, d, 0)
    head = jnp.where(_rows(prev.shape) < d, p, r[:HALO])
    if x.shape[0] == HALO:
        return head
    return jnp.concatenate([head, r[HALO:]], axis=0)


def _causal_conv(x, prev, w, b):
    out = b + x * w[CONV_W - 1:CONV_W]
    for d in range(1, CONV_W):
        out = out + _shift_rows(x, prev, d) * w[CONV_W - 1 - d:CONV_W - d]
    return out


def _tiles(x):
    n, c = x.shape
    return x.reshape(n // SUBLANES, SUBLANES, c)


def _split_bf16(x, terms):
    pieces = []
    rest = x
    for i in range(terms):
        piece = rest.astype(BF16)
        if i + 1 < terms:
            rest = rest - piece.astype(F32)
        pieces.append(piece)
    return pieces


def _cumsum_rows(tri, x):
    n = tri.shape[0]
    segs = []
    for r in range(0, x.shape[0], n):
        out = None
        for piece in _split_bf16(x[r:r + n], 3):
            part = jnp.dot(tri, piece, preferred_element_type=F32)
            out = part if out is None else out + part
        segs.append(out)
    return _cat_rows(segs)


def _linear_scan(a, u, h0):
    a3 = _tiles(a)
    u3 = _tiles(u)
    pos3 = lax.broadcasted_iota(jnp.int32, a3.shape, 1)
    d = 1
    while d < SUBLANES:
        keep = pos3 >= d
        u3 = jnp.where(keep, a3 * pltpu.roll(u3, d, 1) + u3, u3)
        a3 = jnp.where(keep, a3 * pltpu.roll(a3, d, 1), a3)
        d *= 2
    carry = h0
    tiles = []
    for j in range(a3.shape[0]):
        h_j = u3[j] + a3[j] * carry
        carry = h_j[SUBLANES - 1:]
        tiles.append(h_j)
    return jnp.concatenate(tiles, axis=0)


def _pad_rows(x, n):
    if x.shape[0] == n:
        return x
    return jnp.concatenate([x, jnp.zeros((n - x.shape[0], x.shape[1]), x.dtype)], axis=0)


def _tile_rows(x, reps):
    return jnp.concatenate([x] * reps, axis=0)


def _expand_heads(dt_c, acs_c, spread):
    dt_hi, dt_lo = (p.astype(F32) for p in _split_bf16(dt_c, 2))
    acs_hi, acs_lo, acs_lo2 = (p.astype(F32) for p in _split_bf16(acs_c, 3))
    copy = _lanes(dt_c.shape) // B_HEADS
    packed = jnp.where(copy == 0, dt_hi, jnp.where(copy == 1, dt_lo, jnp.where(
        copy == 2, acs_hi, jnp.where(copy == 3, acs_lo, acs_lo2))))
    both = jnp.dot(packed.astype(BF16), spread, preferred_element_type=F32)
    return both[:, :B_WIDTH], both[:, B_WIDTH:]


def _ssd_segment(xdt, acs, bm, cm, st, consts, cs):
    diag_m, causal_m, blockdiag = consts
    n_chunks = xdt.shape[0] // cs
    prep = []
    for c in range(n_chunks):
        sl = slice(c * cs, (c + 1) * cs)
        a_c = acs[sl]
        acs_s = jnp.sum(jnp.where(diag_m, a_c, 0.0), axis=0, keepdims=True)
        decay = jnp.exp2(jnp.where(causal_m, a_c - acs_s, -jnp.inf))
        last = a_c[cs - 1:cs]
        prep.append((sl, a_c, decay, last, xdt[sl] * jnp.exp2(last - a_c)))
    cb, upd = [], []
    for sl, a_c, decay, last, xt in prep:
        for g in range(B_GROUPS):
            ns = slice(g * B_STATE, (g + 1) * B_STATE)
            ws = slice(g * B_GROUP_WIDTH, (g + 1) * B_GROUP_WIDTH)
            cb.append(_dot_nt(cm[sl, ns], _tile_rows(_pad_rows(bm[sl, ns], CHUNK), B_HPG)))
            upd.append(_dot_tn(bm[sl, ns], xt[:, ws]))
    states = [list(st)]
    for c, (sl, a_c, decay, last, xt) in enumerate(prep):
        states.append([jnp.exp2(last[:, g * B_GROUP_WIDTH:(g + 1) * B_GROUP_WIDTH]) * states[c][g]
                       + upd[c * B_GROUPS + g] for g in range(B_GROUPS)])
    y_rows = []
    for c, (sl, a_c, decay, last, xt) in enumerate(prep):
        y_parts = []
        for g in range(B_GROUPS):
            ns = slice(g * B_STATE, (g + 1) * B_STATE)
            ws = slice(g * B_GROUP_WIDTH, (g + 1) * B_GROUP_WIDTH)
            w = cb[c * B_GROUPS + g] * decay[:, ws]
            xbd = jnp.where(blockdiag, _tile_rows(_pad_rows(xdt[sl, ws], CHUNK), B_HPG), 0.0)
            y_parts.append(_dot(w, xbd) + _dot(cm[sl, ns], states[c][g]) * jnp.exp2(a_c[:, ws]))
        y_rows.append(jnp.concatenate(y_parts, axis=1))
    return _cat_rows(y_rows), states[-1]


def _hgrn_segment(q, k, v, bc, st, blockdiag, cs):
    n_chunks = q.shape[0] // cs
    half = cs // 2
    levels = []
    m = SUB
    while m <= half:
        levels.append(m)
        m *= 2
    shape = (C_HEADS * half, C_WIDTH)
    head_m = (_rows(shape) // half) == (_lanes(shape) // C_KDIM)
    jobs, q_in, k_upd, decay_last = [], [], [], []
    for c in range(n_chunks):
        base = c * cs
        bc_c = bc[base:base + cs]
        bl = bc_c[cs - 1:cs]
        q_in.append(q[base:base + cs] * jnp.exp2(bc_c))
        k_upd.append(k[base:base + cs] * jnp.exp2(bl - bc_c))
        decay_last.append(jnp.exp2(bl))
        for m in levels:
            n_blk = cs // (2 * m)
            q_rows, k_rows, v_rows = [], [], []
            for b in range(n_blk):
                lo, mid, hi = base + 2 * m * b, base + 2 * m * b + m, base + 2 * m * (b + 1)
                e = bc[mid - 1:mid]
                k_rows.append(k[lo:mid] * jnp.exp2(e - bc[lo:mid]))
                v_rows.append(v[lo:mid])
                q_rows.append(q[mid:hi] * jnp.exp2(bc[mid:hi] - e))
            kbd = jnp.where(head_m, _tile_rows(_cat_rows(k_rows), C_HEADS), 0.0)
            vbd = jnp.where(head_m, _tile_rows(_cat_rows(v_rows), C_HEADS), 0.0)
            jobs.append((c, m, n_blk, _cat_rows(q_rows), kbd, vbd))
    att = [_dot_nt(q_m, kbd) for (_, _, _, q_m, kbd, _) in jobs]
    upd = [_dot_tn(v[c * cs:(c + 1) * cs], k_upd[c]) for c in range(n_chunks)]
    states = [st]
    for c in range(n_chunks):
        states.append(states[c] * decay_last[c] + jnp.where(blockdiag, upd[c], 0.0))
    pieces = []
    for c in range(n_chunks):
        o_c = _dot_nt(q_in[c], states[c])
        pieces.extend(o_c[i * SUB:(i + 1) * SUB] for i in range(cs // SUB))
    for (c, m, n_blk, _, _, vbd), a in zip(jobs, att):
        if n_blk > 1:
            same = (_rows(a.shape) // m) == ((_lanes(a.shape) % half) // m)
            a = jnp.where(same, a, 0.0)
        o_m = _dot(a, vbd)
        for b in range(n_blk):
            for i in range(m // SUB):
                dst = c * (cs // SUB) + (2 * m * b + m) // SUB + i
                src = b * m + i * SUB
                pieces[dst] = pieces[dst] + o_m[src:src + SUB]
    return _cat_rows(pieces), states[-1]


def _mixer_kernel(layer, nb, ts, cs, has_state, x_ref, *refs):
    refs = list(refs)
    state_in = [refs.pop(0) for _ in range(N_STATES)] if has_state else None
    (n1w_ref, win_ref, acw_ref, acb_ref, wgate_ref, bgate_ref, lam_ref,
     bcw_ref, bcb_ref, dtb_ref, alog_ref, dskip_ref, bnw_ref,
     lb_ref, cnw_ref, wout_ref, tri_ref, spread_ref, ones_ref,
     xo_ref, cao_ref, hao_ref, cbo_ref, sbo_ref, sco_ref,
     ca_s, ha_s, cb_s, sb_s, sc_s) = refs
    t = pl.program_id(1)
    rows = nb * ts
    sq = (B_GROUP_WIDTH, B_GROUP_WIDTH)
    blockdiag = (_rows(sq) // B_HEAD_DIM) == (_lanes(sq) // B_HEAD_DIM)
    hist = slice(HALO - (CONV_W - 1), HALO)

    @pl.when(t == 0)
    def _():
        for ref in (ca_s, ha_s, cb_s, sb_s, sc_s):
            ref[...] = jnp.zeros_like(ref)
        if has_state:
            ca0_ref, ha0_ref, cb0_ref, sb0_ref, sc0_ref = state_in
            ha_s[...] = ha0_ref[...]
            for s in range(nb):
                ca_s[s, hist, :] = ca0_ref[s]
                cb_s[s, hist, :] = cb0_ref[s]
                for g in range(B_GROUPS):
                    sb_s[s, g] = sb0_ref[s, g * B_HPG:(g + 1) * B_HPG].reshape(B_GROUP_WIDTH, B_STATE).T
                kv = sc0_ref[s].reshape(C_WIDTH, C_WIDTH // C_HEADS).T
                sc_s[s] = jnp.where(blockdiag, _tile_rows(kv, C_HEADS), 0.0)

    x = x_ref[0].reshape(rows, D_MODEL)
    h = _rmsnorm(x, n1w_ref[...]).astype(BF16)

    def proj(off, width):
        return jnp.dot(h, win_ref[:, off:off + width], preferred_element_type=F32)

    def seg(v, s):
        return v[s * ts:(s + 1) * ts]

    xa = proj(OFF_XA, A_WIDTH)
    xbc = proj(OFF_XBC, B_CONV_DIM)

    acw = acw_ref[...]
    xc = _cat_rows([_causal_conv(seg(xa, s), ca_s[s], acw, acb_ref[...]) for s in range(nb)])
    for s in range(nb):
        ca_s[s] = seg(xa, s)[ts - HALO:]
    gates = _dot(xc, wgate_ref[...]) + bgate_ref[...]
    dt_raw = proj(OFF_DT, LANES)
    fpre = proj(OFF_F, C_WIDTH)
    q_raw = proj(OFF_Q, C_WIDTH)
    v = proj(OFF_I, C_WIDTH)
    ga = proj(OFF_GA, A_WIDTH)
    z = proj(OFF_Z, B_WIDTH)
    og = proj(OFF_G, C_WIDTH)
    r_gate = jax.nn.sigmoid(gates[:, :A_WIDTH])
    i_gate = jax.nn.sigmoid(gates[:, A_WIDTH:])
    log_a = -LRU_C * r_gate * jax.nn.softplus(-lam_ref[...])
    a = jnp.exp(log_a)
    one_minus_a2 = -jnp.tanh(log_a) * (a * a + 1.0)
    u = jnp.where(one_minus_a2 > 0.0, one_minus_a2 * lax.rsqrt(one_minus_a2), 0.0) * (i_gate * xc)
    ha_parts = []
    for s in range(nb):
        h_seq = _linear_scan(seg(a, s), seg(u, s), ha_s[s])
        ha_s[s] = h_seq[ts - 1:]
        ha_parts.append(h_seq)
    ya = _cat_rows(ha_parts) * jax.nn.gelu(ga)

    bcw = bcw_ref[...]
    xbc_c = _cat_rows([_causal_conv(seg(xbc, s), cb_s[s], bcw, bcb_ref[...]) for s in range(nb)])
    for s in range(nb):
        cb_s[s] = seg(xbc, s)[ts - HALO:]
    xbc_c = jax.nn.silu(xbc_c)
    xs = xbc_c[:, :B_WIDTH]
    bm = xbc_c[:, B_WIDTH:B_WIDTH + B_GROUPS * B_STATE]
    cm = xbc_c[:, B_WIDTH + B_GROUPS * B_STATE:]
    dt_c = jax.nn.softplus(dt_raw + dtb_ref[...])
    acs_c = _cumsum_rows(tri_ref[...], dt_c * (-LOG2E * jnp.exp(alog_ref[...])))
    dt, acs = _expand_heads(dt_c, acs_c, spread_ref[...])
    xdt = xs * dt
    pos512 = _lanes((cs, B_WIDTH)) % B_HEAD_DIM
    row512 = _rows((cs, B_WIDTH))
    ssd_consts = (row512 == pos512, row512 >= pos512, blockdiag)
    yb_parts = []
    for s in range(nb):
        y_s, st = _ssd_segment(seg(xdt, s), seg(acs, s), seg(bm, s), seg(cm, s),
                               [sb_s[s, g] for g in range(B_GROUPS)], ssd_consts, cs)
        yb_parts.append(y_s)
        for g in range(B_GROUPS):
            sb_s[s, g] = st[g]
    yb = _cat_rows(yb_parts) + xs * dskip_ref[...]
    vb = yb * jax.nn.silu(z)
    vb2 = vb * vb
    inv = [jnp.broadcast_to(
        lax.rsqrt(jnp.mean(vb2[:, g * B_GROUP_WIDTH:(g + 1) * B_GROUP_WIDTH], axis=-1, keepdims=True) + EPS),
        (rows, B_GROUP_WIDTH)) for g in range(B_GROUPS)]
    yb = vb * jnp.concatenate(inv, axis=1) * bnw_ref[...]

    lb_all = lb_ref[...]
    lb_e = jnp.exp(lb_all - jnp.max(lb_all, axis=0, keepdims=True))
    lb_sm = lb_e / jnp.sum(lb_e, axis=0, keepdims=True)
    lb_cum = lb_sm[0:1]
    lb_first = lb_cum
    for d in range(1, layer + 1):
        lb_cum = lb_cum + lb_sm[d:d + 1]
    lb = lb_cum - lb_first
    q = jax.nn.silu(q_raw)
    e_f = jnp.exp(-jnp.abs(fpre))
    log_sig = jnp.minimum(fpre, 0.0) - jnp.log(1.0 + e_f)
    sig_neg = jnp.where(fpre > 0.0, e_f, 1.0) / (1.0 + e_f)
    y_lb = jnp.log(lb) - fpre
    log_f = log_sig + jnp.maximum(y_lb, 0.0) + jnp.log(1.0 + jnp.exp(-jnp.abs(y_lb)))
    k = (1.0 - lb) * sig_neg
    bc = _cumsum_rows(tri_ref[...], log_f * LOG2E)
    o_parts = []
    for s in range(nb):
        o_s, sc_s[s] = _hgrn_segment(seg(q, s), seg(k, s), seg(v, s), seg(bc, s), sc_s[s], blockdiag, cs)
        o_parts.append(o_s)
    o3 = _tiles(_cat_rows(o_parts))
    head_ones = ones_ref[...]
    q3, k3, v3, bc3 = _tiles(q), _tiles(k), _tiles(v), _tiles(bc)
    pos3 = lax.broadcasted_iota(jnp.int32, q3.shape, 1)
    for d in range(SUB):
        if d == 0:
            p3 = q3 * k3
            v_d = v3
        else:
            w = jnp.exp2(jnp.where(pos3 >= d, bc3 - pltpu.roll(bc3, d, 1), -jnp.inf))
            p3 = q3 * pltpu.roll(k3, d, 1) * w
            v_d = pltpu.roll(v3, d, 1)
        o3 = o3 + _tiles(_dot(p3.reshape(rows, C_WIDTH), head_ones)) * v_d
    o = o3.reshape(rows, C_WIDTH)
    yc = o * lax.rsqrt(jnp.mean(o * o, axis=-1, keepdims=True) + EPS) * cnw_ref[...] * jax.nn.silu(og)

    mix = jnp.concatenate([ya, yb, yc], axis=1)
    xo_ref[0] = (x + _dot(mix, wout_ref[...])).reshape(nb, ts, D_MODEL)

    @pl.when(t == pl.num_programs(1) - 1)
    def _():
        hao_ref[...] = ha_s[...]
        for s in range(nb):
            cao_ref[s] = ca_s[s, hist, :]
            cbo_ref[s] = cb_s[s, hist, :]
            for g in range(B_GROUPS):
                sbo_ref[s, g * B_HPG:(g + 1) * B_HPG] = sb_s[s, g].T.reshape(B_HPG, B_HEAD_DIM, B_STATE)
            kv = sc_s[s].T
            kv = kv[:, :LANES] + kv[:, LANES:]
            kv = kv + pltpu.roll(kv, C_KDIM, 1)
            sco_ref[s] = kv[:, :C_KDIM].reshape(C_HEADS, C_KDIM, C_KDIM)


def _ffn_kernel(final, x_ref, n2w_ref, wg_ref, wu_ref, wd_ref, fw_ref, o_ref):
    x = x_ref[...]
    h = _rmsnorm(x, n2w_ref[...]).astype(BF16)
    y = x
    for lo in range(0, D_FF, FFN_COLS):
        cols = slice(lo, min(lo + FFN_COLS, D_FF))
        gate = jnp.dot(h, wg_ref[:, cols], preferred_element_type=F32)
        up = jnp.dot(h, wu_ref[:, cols], preferred_element_type=F32)
        y = y + _dot(jax.nn.silu(gate) * up, wd_ref[cols, :])
    if final:
        y = _rmsnorm(y, fw_ref[...])
    o_ref[...] = y


def _const_spec(shape):
    zeros = (0,) * len(shape)
    return pl.BlockSpec(shape, lambda *_: zeros, pipeline_mode=pl.Buffered(1))


def _mixer_call(layer, x, states, weights, nb, ts, cs):
    groups, _, seq_len, _ = x.shape
    n_seqs = groups * nb
    n_steps = seq_len // ts
    x_spec = pl.BlockSpec((1, nb, ts, D_MODEL), lambda g, t: (g, 0, t, 0))
    state_dims = [(CONV_W - 1, A_WIDTH), (1, A_WIDTH), (CONV_W - 1, B_CONV_DIM),
                  (B_HEADS, B_HEAD_DIM, B_STATE), (C_HEADS, C_KDIM, C_KDIM)]
    scratch_dims = [(HALO, A_WIDTH), (1, A_WIDTH), (HALO, B_CONV_DIM),
                    (B_GROUPS, B_STATE, B_GROUP_WIDTH), (C_WIDTH, C_WIDTH)]

    def state_spec(dims):
        tail = (0,) * len(dims)
        return pl.BlockSpec((nb,) + dims, lambda g, t: (g,) + tail)

    state_specs = [state_spec(d) for d in state_dims]
    state_args = [] if states is None else list(states)
    for a, d in zip(state_args, state_dims):
        assert a.shape == (n_seqs,) + d, (a.shape, d)
    return pl.pallas_call(
        functools.partial(_mixer_kernel, layer, nb, ts, cs, states is not None),
        grid=(groups, n_steps),
        in_specs=[x_spec] + state_specs[:len(state_args)] + [_const_spec(w.shape) for w in weights],
        out_specs=[x_spec] + state_specs,
        out_shape=[jax.ShapeDtypeStruct(x.shape, F32)]
        + [jax.ShapeDtypeStruct((n_seqs,) + d, F32) for d in state_dims],
        scratch_shapes=[pltpu.VMEM((nb,) + d, F32) for d in scratch_dims],
        compiler_params=pltpu.CompilerParams(
            dimension_semantics=("arbitrary", "arbitrary"), vmem_limit_bytes=VMEM_LIMIT_BYTES),
        name=f"mixer_l{layer}_nb{nb}",
    )(x, *state_args, *weights)


def _ffn_call(final, x, weights, block_rows, tag):
    n_rows = x.shape[0]
    x_spec = pl.BlockSpec((block_rows, D_MODEL), lambda i: (i, 0))
    return pl.pallas_call(
        functools.partial(_ffn_kernel, final),
        grid=(n_rows // block_rows,),
        in_specs=[x_spec] + [_const_spec(w.shape) for w in weights],
        out_specs=x_spec,
        out_shape=jax.ShapeDtypeStruct(x.shape, F32),
        compiler_params=pltpu.CompilerParams(
            dimension_semantics=("arbitrary",), vmem_limit_bytes=VMEM_LIMIT_BYTES),
        name=f"ffn_{tag}",
    )(x, *weights)


def _block_diag(w):
    heads, n_i, n_j = w.shape
    eye = jnp.eye(heads, dtype=w.dtype)
    return jnp.einsum("hij,hg->higj", w, eye).reshape(heads * n_i, heads * n_j)


def _row(v):
    return v.reshape(1, -1).astype(F32)


def _chunk_tri(rows, chunk):
    t = jnp.arange(rows)[:, None]
    s = jnp.arange(rows)[None, :]
    return ((s <= t) & (s // chunk == t // chunk)).astype(BF16)


def _head_spread():
    r = jnp.arange(LANES)[:, None]
    col = jnp.arange(2 * B_WIDTH)[None, :]
    same_head = (r % B_HEADS) == ((col % B_WIDTH) // B_HEAD_DIM)
    half = jnp.where(r // B_HEADS < 2, 0, 1)
    live = r < HEAD_COPIES * B_HEADS
    return (same_head & (half == col // B_WIDTH) & live).astype(BF16)


def _head_row(v):
    return jnp.pad(_row(jnp.tile(v, HEAD_COPIES)), ((0, 0), (0, LANES - HEAD_COPIES * B_HEADS)))


def kernel(x_prompt, x_sample, state_rglru_conv, state_rglru_h, state_ssd_conv, state_ssd, state_hgrn,
           norm1_w, w_in, rglru_conv_w, rglru_conv_b, rglru_wa, rglru_ba, rglru_wx, rglru_bx, rglru_lambda,
           ssd_conv_w, ssd_conv_b, ssd_dt_bias, ssd_a_log, ssd_d, ssd_norm_w, hgrn_lb, hgrn_norm_w,
           w_out, norm2_w, w_ffn_gate, w_ffn_up, w_ffn_down, final_norm_w):
    bp, seq, _ = x_prompt.shape
    bs, dseq, _ = x_sample.shape
    ffn_rows = min(FFN_ROWS, bp * seq)
    assert seq % PROMPT_ROWS == 0 and PROMPT_ROWS % CHUNK == 0 and (bp * seq) % ffn_rows == 0
    assert dseq <= CHUNK and dseq % (2 * SUB) == 0
    split = [0, A_WIDTH, 2 * A_WIDTH, 2 * A_WIDTH + B_WIDTH, 2 * A_WIDTH + B_WIDTH + B_CONV_DIM]
    split.append(split[-1] + B_HEADS)
    for _ in range(4):
        split.append(split[-1] + C_WIDTH)

    xp = x_prompt
    xs = x_sample.reshape(1, bs, dseq, D_MODEL)
    nbp = PROMPT_SEQS if bp % PROMPT_SEQS == 0 else 1
    p_out = ([], [], [], [], [])
    s_out = ([], [], [], [], [])
    for l in range(DEPTH):
        w = w_in[l]
        pieces = [w[:, split[i]:split[i + 1]] for i in range(9)]
        xa_w, ga_w, z_w, xbc_w, dt_w, q_w, f_w, i_w, g_w = pieces
        win = jnp.concatenate([xa_w, ga_w, z_w, xbc_w, q_w, f_w, i_w, g_w,
                               jnp.pad(jnp.tile(dt_w, (1, HEAD_COPIES)), ((0, 0), (0, LANES - HEAD_COPIES * B_HEADS)))],
                              axis=1).astype(BF16)
        mixer_weights = [
            _row(norm1_w[l]), win,
            rglru_conv_w[l], _row(rglru_conv_b[l]),
            jnp.concatenate([_block_diag(rglru_wa[l]), _block_diag(rglru_wx[l])], axis=1).astype(BF16),
            _row(jnp.concatenate([rglru_ba[l], rglru_bx[l]])), _row(rglru_lambda[l]),
            ssd_conv_w[l], _row(ssd_conv_b[l]), _head_row(ssd_dt_bias[l]), _head_row(ssd_a_log[l]),
            _row(jnp.repeat(ssd_d[l], B_HEAD_DIM)), _row(ssd_norm_w[l]),
            hgrn_lb.astype(F32), _row(hgrn_norm_w[l]), w_out[l].astype(BF16),
        ]
        spread = _head_spread()
        chan_head = jnp.arange(C_WIDTH) // C_KDIM
        head_ones = (chan_head[:, None] == chan_head[None, :]).astype(BF16)
        p_consts = [_chunk_tri(PROMPT_ROWS, CHUNK), spread, head_ones]
        s_consts = [_chunk_tri(bs * dseq, dseq), spread, head_ones]
        ffn_weights = [_row(norm2_w[l]), w_ffn_gate[l].astype(BF16), w_ffn_up[l].astype(BF16),
                       w_ffn_down[l].astype(BF16), _row(final_norm_w)]
        final = l == DEPTH - 1

        xp, *new_p = _mixer_call(l, xp.reshape(bp // nbp, nbp, seq, D_MODEL), None,
                                 mixer_weights + p_consts, nbp, PROMPT_ROWS, CHUNK)
        xp = _ffn_call(final, xp.reshape(bp * seq, D_MODEL), ffn_weights, ffn_rows, f"prompt_l{l}")
        xp = xp.reshape(bp, seq, D_MODEL)

        s_states = [state_rglru_conv[l], state_rglru_h[l].reshape(bs, 1, A_WIDTH), state_ssd_conv[l],
                    state_ssd[l], state_hgrn[l]]
        xs, *new_s = _mixer_call(l, xs, s_states, mixer_weights + s_consts, bs, dseq, dseq)
        xs = _ffn_call(final, xs.reshape(bs * dseq, D_MODEL), ffn_weights, bs * dseq, f"sample_l{l}")
        xs = xs.reshape(1, bs, dseq, D_MODEL)

        for acc, new in ((p_out, new_p), (s_out, new_s)):
            conv_a, h_a, conv_b, s_b, s_c = new
            for dst, val in zip(acc, (conv_a, h_a.reshape(-1, A_WIDTH), conv_b, s_b, s_c)):
                dst.append(val)
    y_prompt = xp
    y_sample = xs.reshape(bs, dseq, D_MODEL)
    return (y_prompt, y_sample, *(jnp.stack(v) for v in p_out), *(jnp.stack(v) for v in s_out))
```

```python
import functools

import jax
import jax.numpy as jnp
from jax import lax
from jax.experimental import pallas as pl
from jax.experimental.pallas import tpu as pltpu

F32 = jnp.float32
BF16 = jnp.bfloat16

D_MODEL = 1024
DEPTH = 2
CHUNK = 64
EPS = 1e-6
LOG2E = 1.4426950408889634
CONV_W = 4
A_WIDTH = 256
A_HEADS = 4
A_HEAD_DIM = A_WIDTH // A_HEADS
LRU_C = 8.0
B_WIDTH = 512
B_HEAD_DIM = 64
B_HEADS = B_WIDTH // B_HEAD_DIM
B_GROUPS = 2
B_HPG = B_HEADS // B_GROUPS
B_STATE = 128
B_GROUP_WIDTH = B_WIDTH // B_GROUPS
B_CONV_DIM = B_WIDTH + 2 * B_GROUPS * B_STATE
C_WIDTH = 256
C_HEADS = 4
C_KDIM = C_WIDTH // C_HEADS
D_FF = 2816
N_STATES = 5

SUBLANES = 8
LANES = 128
HALO = SUBLANES
SUB = SUBLANES
PROMPT_ROWS = 256
PROMPT_SEQS = 4
FFN_ROWS = 1024
FFN_COLS = 768
HEAD_COPIES = 5
VMEM_LIMIT_BYTES = 56 * 1024 * 1024
MIXER_VMEM_LIMIT_BYTES = 62 * 1024 * 1024

OFF_XA = 0
OFF_GA = OFF_XA + A_WIDTH
OFF_Z = OFF_GA + A_WIDTH
OFF_XBC = OFF_Z + B_WIDTH
OFF_Q = OFF_XBC + B_CONV_DIM
OFF_F = OFF_Q + C_WIDTH
OFF_I = OFF_F + C_WIDTH
OFF_G = OFF_I + C_WIDTH
OFF_DT = OFF_G + C_WIDTH
IN_COLS_PADDED = OFF_DT + LANES


def _dot(a, b):
    return jnp.dot(a.astype(BF16), b.astype(BF16), preferred_element_type=F32)


def _dot_nt(a, b):
    return lax.dot_general(a.astype(BF16), b.astype(BF16), (((1,), (1,)), ((), ())),
                           preferred_element_type=F32)


def _dot_tn(a, b):
    return lax.dot_general(a.astype(BF16), b.astype(BF16), (((0,), (0,)), ((), ())),
                           preferred_element_type=F32)


def _rows(shape):
    return lax.broadcasted_iota(jnp.int32, shape, 0)


def _lanes(shape):
    return lax.broadcasted_iota(jnp.int32, shape, 1)


def _rmsnorm(x, w):
    return x * lax.rsqrt(jnp.mean(x * x, axis=-1, keepdims=True) + EPS) * w


def _cat_rows(parts):
    return parts[0] if len(parts) == 1 else jnp.concatenate(parts, axis=0)


def _shift_rows(x, prev, d):
    r = pltpu.roll(x, d, 0)
    p = pltpu.roll(prev, d, 0)
    head = jnp.where(_rows(prev.shape) < d, p, r[:HALO])
    if x.shape[0] == HALO:
        return head
    return jnp.concatenate([head, r[HALO:]], axis=0)


def _causal_conv(x, prev, w, b):
    out = b + x * w[CONV_W - 1:CONV_W]
    for d in range(1, CONV_W):
        out = out + _shift_rows(x, prev, d) * w[CONV_W - 1 - d:CONV_W - d]
    return out


def _tiles(x):
    n, c = x.shape
    return x.reshape(n // SUBLANES, SUBLANES, c)


def _split_bf16(x, terms):
    pieces = []
    rest = x
    for i in range(terms):
        piece = rest.astype(BF16)
        if i + 1 < terms:
            rest = rest - piece.astype(F32)
        pieces.append(piece)
    return pieces


def _cumsum_rows(tri, x):
    n = tri.shape[0]
    segs = []
    for r in range(0, x.shape[0], n):
        out = None
        for piece in _split_bf16(x[r:r + n], 3):
            part = jnp.dot(tri, piece, preferred_element_type=F32)
            out = part if out is None else out + part
        segs.append(out)
    return _cat_rows(segs)


def _linear_scan(a, u, h0):
    a3 = _tiles(a)
    u3 = _tiles(u)
    pos3 = lax.broadcasted_iota(jnp.int32, a3.shape, 1)
    d = 1
    while d < SUBLANES:
        keep = pos3 >= d
        u3 = jnp.where(keep, a3 * pltpu.roll(u3, d, 1) + u3, u3)
        a3 = jnp.where(keep, a3 * pltpu.roll(a3, d, 1), a3)
        d *= 2
    carry = h0
    tiles = []
    for j in range(a3.shape[0]):
        h_j = u3[j] + a3[j] * carry
        carry = h_j[SUBLANES - 1:]
        tiles.append(h_j)
    return jnp.concatenate(tiles, axis=0)


def _pad_rows(x, n):
    if x.shape[0] == n:
        return x
    return jnp.concatenate([x, jnp.zeros((n - x.shape[0], x.shape[1]), x.dtype)], axis=0)


def _tile_rows(x, reps):
    return jnp.concatenate([x] * reps, axis=0)


def _expand_heads(dt_c, acs_c, spread):
    dt_hi, dt_lo = (p.astype(F32) for p in _split_bf16(dt_c, 2))
    acs_hi, acs_lo, acs_lo2 = (p.astype(F32) for p in _split_bf16(acs_c, 3))
    copy = _lanes(dt_c.shape) // B_HEADS
    packed = jnp.where(copy == 0, dt_hi, jnp.where(copy == 1, dt_lo, jnp.where(
        copy == 2, acs_hi, jnp.where(copy == 3, acs_lo, acs_lo2))))
    both = jnp.dot(packed.astype(BF16), spread, preferred_element_type=F32)
    return both[:, :B_WIDTH], both[:, B_WIDTH:]


def _ssd_segment(xdt, acs, bm, cm, st, consts, cs):
    diag_m, causal_m, blockdiag = consts
    n_chunks = xdt.shape[0] // cs
    prep = []
    for c in range(n_chunks):
        sl = slice(c * cs, (c + 1) * cs)
        a_c = acs[sl]
        acs_s = jnp.sum(jnp.where(diag_m, a_c, 0.0), axis=0, keepdims=True)
        decay = jnp.exp2(jnp.where(causal_m, a_c - acs_s, -jnp.inf))
        last = a_c[cs - 1:cs]
        prep.append((sl, a_c, decay, last, xdt[sl] * jnp.exp2(last - a_c)))
    cb, upd = [], []
    for sl, a_c, decay, last, xt in prep:
        for g in range(B_GROUPS):
            ns = slice(g * B_STATE, (g + 1) * B_STATE)
            ws = slice(g * B_GROUP_WIDTH, (g + 1) * B_GROUP_WIDTH)
            cb.append(_dot_nt(cm[sl, ns], _tile_rows(_pad_rows(bm[sl, ns], CHUNK), B_HPG)))
            upd.append(_dot_tn(bm[sl, ns], xt[:, ws]))
    states = [list(st)]
    for c, (sl, a_c, decay, last, xt) in enumerate(prep):
        states.append([jnp.exp2(last[:, g * B_GROUP_WIDTH:(g + 1) * B_GROUP_WIDTH]) * states[c][g]
                       + upd[c * B_GROUPS + g] for g in range(B_GROUPS)])
    y_rows = []
    for c, (sl, a_c, decay, last, xt) in enumerate(prep):
        y_parts = []
        for g in range(B_GROUPS):
            ns = slice(g * B_STATE, (g + 1) * B_STATE)
            ws = slice(g * B_GROUP_WIDTH, (g + 1) * B_GROUP_WIDTH)
            w = cb[c * B_GROUPS + g] * decay[:, ws]
            xbd = jnp.where(blockdiag, _tile_rows(_pad_rows(xdt[sl, ws], CHUNK), B_HPG), 0.0)
            y_parts.append(_dot(w, xbd) + _dot(cm[sl, ns], states[c][g]) * jnp.exp2(a_c[:, ws]))
        y_rows.append(jnp.concatenate(y_parts, axis=1))
    return _cat_rows(y_rows), states[-1]


def _hgrn_segment(q, k, v, bc, st, blockdiag, cs):
    n_chunks = q.shape[0] // cs
    half = cs // 2
    levels = []
    m = SUB
    while m <= half:
        levels.append(m)
        m *= 2
    shape = (C_HEADS * half, C_WIDTH)
    head_m = (_rows(shape) // half) == (_lanes(shape) // C_KDIM)
    jobs, q_in, k_upd, decay_last = [], [], [], []
    for c in range(n_chunks):
        base = c * cs
        bc_c = bc[base:base + cs]
        bl = bc_c[cs - 1:cs]
        q_in.append(q[base:base + cs] * jnp.exp2(bc_c))
        k_upd.append(k[base:base + cs] * jnp.exp2(bl - bc_c))
        decay_last.append(jnp.exp2(bl))
        for m in levels:
            n_blk = cs // (2 * m)
            q_rows, k_rows, v_rows = [], [], []
            for b in range(n_blk):
                lo, mid, hi = base + 2 * m * b, base + 2 * m * b + m, base + 2 * m * (b + 1)
                e = bc[mid - 1:mid]
                k_rows.append(k[lo:mid] * jnp.exp2(e - bc[lo:mid]))
                v_rows.append(v[lo:mid])
                q_rows.append(q[mid:hi] * jnp.exp2(bc[mid:hi] - e))
            kbd = jnp.where(head_m, _tile_rows(_cat_rows(k_rows), C_HEADS), 0.0)
            vbd = jnp.where(head_m, _tile_rows(_cat_rows(v_rows), C_HEADS), 0.0)
            jobs.append((c, m, n_blk, _cat_rows(q_rows), kbd, vbd))
    att = [_dot_nt(q_m, kbd) for (_, _, _, q_m, kbd, _) in jobs]
    upd = [_dot_tn(v[c * cs:(c + 1) * cs], k_upd[c]) for c in range(n_chunks)]
    states = [st]
    for c in range(n_chunks):
        states.append(states[c] * decay_last[c] + jnp.where(blockdiag, upd[c], 0.0))
    pieces = []
    for c in range(n_chunks):
        o_c = _dot_nt(q_in[c], states[c])
        pieces.extend(o_c[i * SUB:(i + 1) * SUB] for i in range(cs // SUB))
    for (c, m, n_blk, _, _, vbd), a in zip(jobs, att):
        if n_blk > 1:
            same = (_rows(a.shape) // m) == ((_lanes(a.shape) % half) // m)
            a = jnp.where(same, a, 0.0)
        o_m = _dot(a, vbd)
        for b in range(n_blk):
            for i in range(m // SUB):
                dst = c * (cs // SUB) + (2 * m * b + m) // SUB + i
                src = b * m + i * SUB
                pieces[dst] = pieces[dst] + o_m[src:src + SUB]
    return _cat_rows(pieces), states[-1]


def _mixer_kernel(layer, nb, ts, cs, has_state, x_ref, *refs):
    refs = list(refs)
    state_in = [refs.pop(0) for _ in range(N_STATES)] if has_state else None
    (n1w_ref, win_ref, acw_ref, acb_ref, wgate_ref, bgate_ref, lam_ref,
     bcw_ref, bcb_ref, dtb_ref, alog_ref, dskip_ref, bnw_ref,
     lb_ref, cnw_ref, wout_ref, tri_ref, spread_ref, ones_ref,
     xo_ref, cao_ref, hao_ref, cbo_ref, sbo_ref, sco_ref,
     ca_s, ha_s, cb_s, sb_s, sc_s) = refs
    t = pl.program_id(1)
    rows = nb * ts
    sq = (B_GROUP_WIDTH, B_GROUP_WIDTH)
    blockdiag = (_rows(sq) // B_HEAD_DIM) == (_lanes(sq) // B_HEAD_DIM)
    hist = slice(HALO - (CONV_W - 1), HALO)

    @pl.when(t == 0)
    def _():
        for ref in (ca_s, ha_s, cb_s, sb_s, sc_s):
            ref[...] = jnp.zeros_like(ref)
        if has_state:
            ca0_ref, ha0_ref, cb0_ref, sb0_ref, sc0_ref = state_in
            ha_s[...] = ha0_ref[...]
            for s in range(nb):
                ca_s[s, hist, :] = ca0_ref[s]
                cb_s[s, hist, :] = cb0_ref[s]
                for g in range(B_GROUPS):
                    sb_s[s, g] = sb0_ref[s, g * B_HPG:(g + 1) * B_HPG].reshape(B_GROUP_WIDTH, B_STATE).T
                kv = sc0_ref[s].reshape(C_WIDTH, C_WIDTH // C_HEADS).T
                sc_s[s] = jnp.where(blockdiag, _tile_rows(kv, C_HEADS), 0.0)

    x = x_ref[0].reshape(rows, D_MODEL)
    h = _rmsnorm(x, n1w_ref[...]).astype(BF16)

    def proj(off, width):
        return jnp.dot(h, win_ref[:, off:off + width], preferred_element_type=F32)

    def seg(v, s):
        return v[s * ts:(s + 1) * ts]

    xa = proj(OFF_XA, A_WIDTH)
    xbc = proj(OFF_XBC, B_CONV_DIM)

    acw = acw_ref[...]
    xc = _cat_rows([_causal_conv(seg(xa, s), ca_s[s], acw, acb_ref[...]) for s in range(nb)])
    for s in range(nb):
        ca_s[s] = seg(xa, s)[ts - HALO:]
    gates = _dot(xc, wgate_ref[...]) + bgate_ref[...]
    dt_raw = proj(OFF_DT, LANES)
    fpre = proj(OFF_F, C_WIDTH)
    q_raw = proj(OFF_Q, C_WIDTH)
    v = proj(OFF_I, C_WIDTH)
    ga = proj(OFF_GA, A_WIDTH)
    z = proj(OFF_Z, B_WIDTH)
    og = proj(OFF_G, C_WIDTH)
    r_gate = jax.nn.sigmoid(gates[:, :A_WIDTH])
    i_gate = jax.nn.sigmoid(gates[:, A_WIDTH:])
    log_a = -LRU_C * r_gate * jax.nn.softplus(-lam_ref[...])
    a = jnp.exp(log_a)
    one_minus_a2 = -jnp.tanh(log_a) * (a * a + 1.0)
    u = jnp.where(one_minus_a2 > 0.0, one_minus_a2 * lax.rsqrt(one_minus_a2), 0.0) * (i_gate * xc)
    ha_parts = []
    for s in range(nb):
        h_seq = _linear_scan(seg(a, s), seg(u, s), ha_s[s])
        ha_s[s] = h_seq[ts - 1:]
        ha_parts.append(h_seq)
    ya = _cat_rows(ha_parts) * jax.nn.gelu(ga)

    bcw = bcw_ref[...]
    xbc_c = _cat_rows([_causal_conv(seg(xbc, s), cb_s[s], bcw, bcb_ref[...]) for s in range(nb)])
    for s in range(nb):
        cb_s[s] = seg(xbc, s)[ts - HALO:]
    xbc_c = jax.nn.silu(xbc_c)
    xs = xbc_c[:, :B_WIDTH]
    bm = xbc_c[:, B_WIDTH:B_WIDTH + B_GROUPS * B_STATE]
    cm = xbc_c[:, B_WIDTH + B_GROUPS * B_STATE:]
    dt_c = jax.nn.softplus(dt_raw + dtb_ref[...])
    acs_c = _cumsum_rows(tri_ref[...], dt_c * (-LOG2E * jnp.exp(alog_ref[...])))
    dt, acs = _expand_heads(dt_c, acs_c, spread_ref[...])
    xdt = xs * dt
    pos512 = _lanes((cs, B_WIDTH)) % B_HEAD_DIM
    row512 = _rows((cs, B_WIDTH))
    ssd_consts = (row512 == pos512, row512 >= pos512, blockdiag)
    yb_parts = []
    for s in range(nb):
        y_s, st = _ssd_segment(seg(xdt, s), seg(acs, s), seg(bm, s), seg(cm, s),
                               [sb_s[s, g] for g in range(B_GROUPS)], ssd_consts, cs)
        yb_parts.append(y_s)
        for g in range(B_GROUPS):
            sb_s[s, g] = st[g]
    yb = _cat_rows(yb_parts) + xs * dskip_ref[...]
    vb = yb * jax.nn.silu(z)
    vb2 = vb * vb
    inv = [jnp.broadcast_to(
        lax.rsqrt(jnp.mean(vb2[:, g * B_GROUP_WIDTH:(g + 1) * B_GROUP_WIDTH], axis=-1, keepdims=True) + EPS),
        (rows, B_GROUP_WIDTH)) for g in range(B_GROUPS)]
    yb = vb * jnp.concatenate(inv, axis=1) * bnw_ref[...]

    lb_all = lb_ref[...]
    lb_e = jnp.exp(lb_all - jnp.max(lb_all, axis=0, keepdims=True))
    lb_sm = lb_e / jnp.sum(lb_e, axis=0, keepdims=True)
    lb_cum = lb_sm[0:1]
    lb_first = lb_cum
    for d in range(1, layer + 1):
        lb_cum = lb_cum + lb_sm[d:d + 1]
    lb = lb_cum - lb_first
    q = jax.nn.silu(q_raw)
    e_f = jnp.exp(-jnp.abs(fpre))
    log_sig = jnp.minimum(fpre, 0.0) - jnp.log(1.0 + e_f)
    sig_neg = jnp.where(fpre > 0.0, e_f, 1.0) / (1.0 + e_f)
    y_lb = jnp.log(lb) - fpre
    log_f = log_sig + jnp.maximum(y_lb, 0.0) + jnp.log(1.0 + jnp.exp(-jnp.abs(y_lb)))
    k = (1.0 - lb) * sig_neg
    bc = _cumsum_rows(tri_ref[...], log_f * LOG2E)
    o_parts = []
    for s in range(nb):
        o_s, sc_s[s] = _hgrn_segment(seg(q, s), seg(k, s), seg(v, s), seg(bc, s), sc_s[s], blockdiag, cs)
        o_parts.append(o_s)
    o3 = _tiles(_cat_rows(o_parts))
    head_ones = ones_ref[...]
    q3, k3, v3, bc3 = _tiles(q), _tiles(k), _tiles(v), _tiles(bc)
    pos3 = lax.broadcasted_iota(jnp.int32, q3.shape, 1)
    for d in range(SUB):
        if d == 0:
            p3 = q3 * k3
            v_d = v3
        else:
            w = jnp.exp2(jnp.where(pos3 >= d, bc3 - pltpu.roll(bc3, d, 1), -jnp.inf))
            p3 = q3 * pltpu.roll(k3, d, 1) * w
            v_d = pltpu.roll(v3, d, 1)
        o3 = o3 + _tiles(_dot(p3.reshape(rows, C_WIDTH), head_ones)) * v_d
    o = o3.reshape(rows, C_WIDTH)
    yc = o * lax.rsqrt(jnp.mean(o * o, axis=-1, keepdims=True) + EPS) * cnw_ref[...] * jax.nn.silu(og)

    mix = jnp.concatenate([ya, yb, yc], axis=1)
    xo_ref[0] = (x + _dot(mix, wout_ref[...])).reshape(nb, ts, D_MODEL)

    @pl.when(t == pl.num_programs(1) - 1)
    def _():
        hao_ref[...] = ha_s[...]
        for s in range(nb):
            cao_ref[s] = ca_s[s, hist, :]
            cbo_ref[s] = cb_s[s, hist, :]
            for g in range(B_GROUPS):
                sbo_ref[s, g * B_HPG:(g + 1) * B_HPG] = sb_s[s, g].T.reshape(B_HPG, B_HEAD_DIM, B_STATE)
            kv = sc_s[s].T
            kv = kv[:, :LANES] + kv[:, LANES:]
            kv = kv + pltpu.roll(kv, C_KDIM, 1)
            sco_ref[s] = kv[:, :C_KDIM].reshape(C_HEADS, C_KDIM, C_KDIM)


def _ffn_kernel(final, x_ref, n2w_ref, wg_ref, wu_ref, wd_ref, fw_ref, o_ref):
    x = x_ref[...]
    h = _rmsnorm(x, n2w_ref[...]).astype(BF16)
    y = x
    for lo in range(0, D_FF, FFN_COLS):
        cols = slice(lo, min(lo + FFN_COLS, D_FF))
        gate = jnp.dot(h, wg_ref[:, cols], preferred_element_type=F32)
        up = jnp.dot(h, wu_ref[:, cols], preferred_element_type=F32)
        y = y + _dot(jax.nn.silu(gate) * up, wd_ref[cols, :])
    if final:
        y = _rmsnorm(y, fw_ref[...])
    o_ref[...] = y


def _const_spec(shape):
    zeros = (0,) * len(shape)
    return pl.BlockSpec(shape, lambda *_: zeros, pipeline_mode=pl.Buffered(1))


def _mixer_call(layer, x, states, weights, nb, ts, cs):
    groups, _, seq_len, _ = x.shape
    n_seqs = groups * nb
    n_steps = seq_len // ts
    x_spec = pl.BlockSpec((1, nb, ts, D_MODEL), lambda g, t: (g, 0, t, 0))
    state_dims = [(CONV_W - 1, A_WIDTH), (1, A_WIDTH), (CONV_W - 1, B_CONV_DIM),
                  (B_HEADS, B_HEAD_DIM, B_STATE), (C_HEADS, C_KDIM, C_KDIM)]
    scratch_dims = [(HALO, A_WIDTH), (1, A_WIDTH), (HALO, B_CONV_DIM),
                    (B_GROUPS, B_STATE, B_GROUP_WIDTH), (C_WIDTH, C_WIDTH)]

    def state_spec(dims):
        tail = (0,) * len(dims)
        return pl.BlockSpec((nb,) + dims, lambda g, t: (g,) + tail)

    state_specs = [state_spec(d) for d in state_dims]
    state_args = [] if states is None else list(states)
    for a, d in zip(state_args, state_dims):
        assert a.shape == (n_seqs,) + d, (a.shape, d)
    return pl.pallas_call(
        functools.partial(_mixer_kernel, layer, nb, ts, cs, states is not None),
        grid=(groups, n_steps),
        in_specs=[x_spec] + state_specs[:len(state_args)] + [_const_spec(w.shape) for w in weights],
        out_specs=[x_spec] + state_specs,
        out_shape=[jax.ShapeDtypeStruct(x.shape, F32)]
        + [jax.ShapeDtypeStruct((n_seqs,) + d, F32) for d in state_dims],
        scratch_shapes=[pltpu.VMEM((nb,) + d, F32) for d in scratch_dims],
        compiler_params=pltpu.CompilerParams(
            dimension_semantics=("arbitrary", "arbitrary"), vmem_limit_bytes=MIXER_VMEM_LIMIT_BYTES),
        name=f"mixer_l{layer}_nb{nb}",
    )(x, *state_args, *weights)


def _ffn_call(final, x, weights, block_rows, tag):
    n_rows = x.shape[0]
    x_spec = pl.BlockSpec((block_rows, D_MODEL), lambda i: (i, 0))
    return pl.pallas_call(
        functools.partial(_ffn_kernel, final),
        grid=(n_rows // block_rows,),
        in_specs=[x_spec] + [_const_spec(w.shape) for w in weights],
        out_specs=x_spec,
        out_shape=jax.ShapeDtypeStruct(x.shape, F32),
        compiler_params=pltpu.CompilerParams(
            dimension_semantics=("arbitrary",), vmem_limit_bytes=VMEM_LIMIT_BYTES),
        name=f"ffn_{tag}",
    )(x, *weights)


def _block_diag(w):
    heads, n_i, n_j = w.shape
    eye = jnp.eye(heads, dtype=w.dtype)
    return jnp.einsum("hij,hg->higj", w, eye).reshape(heads * n_i, heads * n_j)


def _row(v):
    return v.reshape(1, -1).astype(F32)


def _chunk_tri(rows, chunk):
    t = jnp.arange(rows)[:, None]
    s = jnp.arange(rows)[None, :]
    return ((s <= t) & (s // chunk == t // chunk)).astype(BF16)


def _head_spread():
    r = jnp.arange(LANES)[:, None]
    col = jnp.arange(2 * B_WIDTH)[None, :]
    same_head = (r % B_HEADS) == ((col % B_WIDTH) // B_HEAD_DIM)
    half = jnp.where(r // B_HEADS < 2, 0, 1)
    live = r < HEAD_COPIES * B_HEADS
    return (same_head & (half == col // B_WIDTH) & live).astype(BF16)


def _head_row(v):
    return jnp.pad(_row(jnp.tile(v, HEAD_COPIES)), ((0, 0), (0, LANES - HEAD_COPIES * B_HEADS)))


def kernel(x_prompt, x_sample, state_rglru_conv, state_rglru_h, state_ssd_conv, state_ssd, state_hgrn,
           norm1_w, w_in, rglru_conv_w, rglru_conv_b, rglru_wa, rglru_ba, rglru_wx, rglru_bx, rglru_lambda,
           ssd_conv_w, ssd_conv_b, ssd_dt_bias, ssd_a_log, ssd_d, ssd_norm_w, hgrn_lb, hgrn_norm_w,
           w_out, norm2_w, w_ffn_gate, w_ffn_up, w_ffn_down, final_norm_w):
    bp, seq, _ = x_prompt.shape
    bs, dseq, _ = x_sample.shape
    ffn_rows = min(FFN_ROWS, bp * seq)
    assert seq % PROMPT_ROWS == 0 and PROMPT_ROWS % CHUNK == 0 and (bp * seq) % ffn_rows == 0
    assert dseq <= CHUNK and dseq % (2 * SUB) == 0
    split = [0, A_WIDTH, 2 * A_WIDTH, 2 * A_WIDTH + B_WIDTH, 2 * A_WIDTH + B_WIDTH + B_CONV_DIM]
    split.append(split[-1] + B_HEADS)
    for _ in range(4):
        split.append(split[-1] + C_WIDTH)

    xp = x_prompt
    xs = x_sample.reshape(1, bs, dseq, D_MODEL)
    nbp = PROMPT_SEQS if bp % PROMPT_SEQS == 0 else 1
    p_out = ([], [], [], [], [])
    s_out = ([], [], [], [], [])
    for l in range(DEPTH):
        w = w_in[l]
        pieces = [w[:, split[i]:split[i + 1]] for i in range(9)]
        xa_w, ga_w, z_w, xbc_w, dt_w, q_w, f_w, i_w, g_w = pieces
        win = jnp.concatenate([xa_w, ga_w, z_w, xbc_w, q_w, f_w, i_w, g_w,
                               jnp.pad(jnp.tile(dt_w, (1, HEAD_COPIES)), ((0, 0), (0, LANES - HEAD_COPIES * B_HEADS)))],
                              axis=1).astype(BF16)
        mixer_weights = [
            _row(norm1_w[l]), win,
            rglru_conv_w[l], _row(rglru_conv_b[l]),
            jnp.concatenate([_block_diag(rglru_wa[l]), _block_diag(rglru_wx[l])], axis=1).astype(BF16),
            _row(jnp.concatenate([rglru_ba[l], rglru_bx[l]])), _row(rglru_lambda[l]),
            ssd_conv_w[l], _row(ssd_conv_b[l]), _head_row(ssd_dt_bias[l]), _head_row(ssd_a_log[l]),
            _row(jnp.repeat(ssd_d[l], B_HEAD_DIM)), _row(ssd_norm_w[l]),
            hgrn_lb.astype(F32), _row(hgrn_norm_w[l]), w_out[l].astype(BF16),
        ]
        spread = _head_spread()
        chan_head = jnp.arange(C_WIDTH) // C_KDIM
        head_ones = (chan_head[:, None] == chan_head[None, :]).astype(BF16)
        p_consts = [_chunk_tri(PROMPT_ROWS, CHUNK), spread, head_ones]
        s_consts = [_chunk_tri(bs * dseq, dseq), spread, head_ones]
        ffn_weights = [_row(norm2_w[l]), w_ffn_gate[l].astype(BF16), w_ffn_up[l].astype(BF16),
                       w_ffn_down[l].astype(BF16), _row(final_norm_w)]
        final = l == DEPTH - 1

        xp, *new_p = _mixer_call(l, xp.reshape(bp // nbp, nbp, seq, D_MODEL), None,
                                 mixer_weights + p_consts, nbp, PROMPT_ROWS, CHUNK)
        xp = _ffn_call(final, xp.reshape(bp * seq, D_MODEL), ffn_weights, ffn_rows, f"prompt_l{l}")
        xp = xp.reshape(bp, seq, D_MODEL)

        s_states = [state_rglru_conv[l], state_rglru_h[l].reshape(bs, 1, A_WIDTH), state_ssd_conv[l],
                    state_ssd[l], state_hgrn[l]]
        xs, *new_s = _mixer_call(l, xs, s_states, mixer_weights + s_consts, bs, dseq, dseq)
        xs = _ffn_call(final, xs.reshape(bs * dseq, D_MODEL), ffn_weights, bs * dseq, f"sample_l{l}")
        xs = xs.reshape(1, bs, dseq, D_MODEL)

        for acc, new in ((p_out, new_p), (s_out, new_s)):
            conv_a, h_a, conv_b, s_b, s_c = new
            for dst, val in zip(acc, (conv_a, h_a.reshape(-1, A_WIDTH), conv_b, s_b, s_c)):
                dst.append(val)
    y_prompt = xp
    y_sample = xs.reshape(bs, dseq, D_MODEL)
    return (y_prompt, y_sample, *(jnp.stack(v) for v in p_out), *(jnp.stack(v) for v in s_out))
```
